```python
import math
import jax
import jax.numpy as jnp
from jax import lax
import numpy as np

D_MODEL = 1024
BATCH = 4
SEQ = 8192
DEPTH = 2

GRID_W = 64
CTX_LEN = 256
EPS = 1e-6

CONV_DIM = 512
CONV_W = 3
GDN_HEADS = 4
GDN_DK = 128
GDN_DV = 128
GDN_CHUNK = 64
GDN_QK = GDN_HEADS * GDN_DK
GDN_V = GDN_HEADS * GDN_DV
MIX_DIM = CONV_DIM + GDN_V
IN_SIZES = (CONV_DIM, CONV_DIM, CONV_DIM, 2 * GDN_QK + GDN_V, GDN_V, GDN_HEADS, GDN_HEADS, GDN_HEADS, GDN_HEADS)
IN_COLS = sum(IN_SIZES)
DIFF_HEADS = 8
DIFF_DH = 64
DIFF_QK = DIFF_HEADS * 2 * DIFF_DH
DIFF_V = DIFF_HEADS * 2 * DIFF_DH
Q_BLOCK = 128
ROPE_BASE = 10000.0
N_EXPERTS = 16
EXPERT_FF = 2048
CAPACITY_FACTOR = 2

kernel_name = 'hybrid_dit_conv_gdn_diffattn_ecmoe'


def _rmsnorm(x, w):
    xf = x.astype(jnp.float32)
    y = xf * lax.rsqrt(jnp.mean(xf * xf, axis=-1, keepdims=True) + EPS)
    return y.astype(x.dtype) * w


def _l2norm(x):
    xf = x.astype(jnp.float32)
    return (xf * lax.rsqrt(jnp.sum(xf * xf, axis=-1, keepdims=True) + EPS)).astype(x.dtype)


def _adaln(cond, w, b):
    m = jax.nn.silu(cond) @ w + b
    return jnp.split(m[:, None, :], 6, axis=-1)


def _modulate(x, w, shift, scale):
    return _rmsnorm(x, w) * (1.0 + scale) + shift


def _dwconv_centred(x, w):
    pad = CONV_W // 2
    return lax.conv_general_dilated(x, w[:, None, :], window_strides=(1,), padding=[(pad, pad)],
                                    dimension_numbers=('NWC', 'WIO', 'NWC'),
                                    feature_group_count=x.shape[-1])


def _gated_delta_chunked(q, k, v, g, beta, s0):
    f32 = jnp.float32
    bn, seq_len, nh, dk = q.shape
    dv = v.shape[-1]
    n = seq_len // GDN_CHUNK

    def chunks(t):
        t = t.astype(f32).reshape((bn, n, GDN_CHUNK, nh) + t.shape[3:])
        return jnp.moveaxis(t, 3, 1)

    q, k, v, g, beta = (chunks(t) for t in (q, k, v, g, beta))
    gc = jnp.cumsum(g, axis=-1)
    incl = jnp.tril(jnp.ones((GDN_CHUNK, GDN_CHUNK), bool))
    strict = jnp.tril(jnp.ones((GDN_CHUNK, GDN_CHUNK), bool), k=-1)
    diff = gc[..., :, None] - gc[..., None, :]
    decay = jnp.where(incl, jnp.exp(jnp.where(incl, diff, 0.0)), 0.0)
    k_beta = k * beta[..., None]
    lower = jnp.where(strict, jnp.einsum('bhncd,bhnmd->bhncm', k_beta, k) * decay, 0.0)
    eye = jnp.eye(GDN_CHUNK, dtype=f32)
    t_mat = lax.linalg.triangular_solve(eye + lower, jnp.broadcast_to(eye, lower.shape),
                                        left_side=True, lower=True, unit_diagonal=True)
    u = jnp.einsum('bhncm,bhnme->bhnce', t_mat, v * beta[..., None])
    w = jnp.einsum('bhncm,bhnmd->bhncd', t_mat, k_beta * jnp.exp(gc)[..., None])
    intra = jnp.where(incl, jnp.einsum('bhncd,bhnmd->bhncm', q, k) * decay, 0.0)
    q_dec = q * jnp.exp(gc)[..., None]
    k_dec = k * jnp.exp(gc[..., -1:] - gc)[..., None]
    g_last = jnp.exp(gc[..., -1])

    def step(state, xs):
        q_i, k_i, u_i, w_i, a_i, gl_i = xs
        v_new = u_i - jnp.einsum('bhcd,bhde->bhce', w_i, state)
        o = jnp.einsum('bhcd,bhde->bhce', q_i, state) + jnp.einsum('bhcm,bhme->bhce', a_i, v_new)
        state = state * gl_i[..., None, None] + jnp.einsum('bhcd,bhce->bhde', k_i, v_new)
        return state, o

    xs = tuple(jnp.moveaxis(t, 2, 0) for t in (q_dec, k_dec, u, w, intra, g_last))
    state, o = lax.scan(step, s0.astype(f32), xs)
    o = jnp.transpose(o, (1, 0, 3, 2, 4)).reshape(bn, seq_len, nh, dv)
    return o, state


def _conv_gdn_mixer(h, p, s0_f, s0_b):
    bn, seq_len, _ = h.shape
    splits = [int(s) for s in np.cumsum(IN_SIZES)[:-1]]
    xa, gate_b, gate_c, qkv, z, a_f, a_b, b_f, b_b = jnp.split(h @ p['w_in'], splits, axis=-1)
    y_conv = gate_b * _dwconv_centred(gate_c * xa, p['conv_a'])
    qkv = jax.nn.silu(_dwconv_centred(qkv, p['conv_qkv']))
    q, k, v = jnp.split(qkv, [GDN_QK, 2 * GDN_QK], axis=-1)
    q = _l2norm(q.reshape(bn, seq_len, GDN_HEADS, GDN_DK)) * GDN_DK ** -0.5
    k = _l2norm(k.reshape(bn, seq_len, GDN_HEADS, GDN_DK))
    v = v.reshape(bn, seq_len, GDN_HEADS, GDN_DV)

    def log_decay(a, d):
        return -jnp.exp(p['a_log'][d].astype(jnp.float32)) * jax.nn.softplus(a.astype(jnp.float32) + p['dt_bias'][d])

    def flip(t):
        return jnp.flip(t, axis=1)

    o_f, s_f = _gated_delta_chunked(q, k, v, log_decay(a_f, 0), jax.nn.sigmoid(b_f), s0_f)
    o_b, s_b = _gated_delta_chunked(flip(q), flip(k), flip(v), flip(log_decay(a_b, 1)),
                                    flip(jax.nn.sigmoid(b_b)), s0_b)
    o = o_f + flip(o_b)
    y_gdn = _rmsnorm(o, p['gdn_norm']) * jax.nn.silu(z.reshape(bn, seq_len, GDN_HEADS, GDN_DV).astype(jnp.float32))
    y = jnp.concatenate([y_conv, y_gdn.reshape(bn, seq_len, GDN_V).astype(h.dtype)], axis=-1) @ p['w_out']
    return y, s_f, s_b


def _axial_angles(seq_len):
    rows = seq_len // GRID_W
    row = jnp.repeat(jnp.arange(rows), GRID_W).astype(jnp.float32)
    col = jnp.tile(jnp.arange(GRID_W), rows).astype(jnp.float32)
    n_freq = DIFF_DH // 4
    inv_freq = ROPE_BASE ** (-jnp.arange(n_freq, dtype=jnp.float32) / n_freq)
    return row[:, None] * inv_freq, col[:, None] * inv_freq


def _rope_1d(x, ang):
    cos = jnp.cos(ang)[None, :, None, None, :].astype(x.dtype)
    sin = jnp.sin(ang)[None, :, None, None, :].astype(x.dtype)
    x1, x2 = jnp.split(x, 2, axis=-1)
    return jnp.concatenate([x1 * cos - x2 * sin, x2 * cos + x1 * sin], axis=-1)


def _axial_rope(x, ang_row, ang_col):
    half = DIFF_DH // 2
    return jnp.concatenate([_rope_1d(x[..., :half], ang_row), _rope_1d(x[..., half:], ang_col)], axis=-1)


def _diff_attention(q, k, v, lam):
    bn, lq = q.shape[:2]
    nb = lq // Q_BLOCK
    qb = jnp.moveaxis(q.reshape((bn, nb, Q_BLOCK) + q.shape[2:]), 1, 0)
    scale = DIFF_DH ** -0.5

    def block(qi):
        s = jnp.einsum('bqhid,bkhid->bihqk', qi, k).astype(jnp.float32) * scale
        pr = jax.nn.softmax(s, axis=-1)
        a = pr[:, 0] - lam * pr[:, 1]
        return jnp.einsum('bhqk,bkhe->bqhe', a, v)

    o = lax.map(block, qb)
    return jnp.moveaxis(o, 0, 1).reshape(bn, lq, q.shape[2], v.shape[-1])


def _diff_attn_mixer(h_lat, h_ctx, p, lam_init, need_ctx):
    bn, seq_len, _ = h_lat.shape

    def qk_heads(t):
        return t.reshape(t.shape[:2] + (DIFF_HEADS, 2, DIFF_DH))

    def v_heads(t):
        return t.reshape(t.shape[:2] + (DIFF_HEADS, 2 * DIFF_DH))

    q_lat, k_lat, v_lat = jnp.split(h_lat @ p['w_qkv'], [DIFF_QK, 2 * DIFF_QK], axis=-1)
    ang_row, ang_col = _axial_angles(seq_len)
    q_lat = _axial_rope(qk_heads(q_lat), ang_row, ang_col)
    k_lat = _axial_rope(qk_heads(k_lat), ang_row, ang_col)
    k_ctx, v_ctx = jnp.split(h_ctx @ p['w_qkv'][:, DIFF_QK:], [DIFF_QK], axis=-1)
    k_ctx, v_ctx = qk_heads(k_ctx), v_heads(v_ctx)
    lam = (jnp.exp(jnp.sum(p['lambda_q1'] * p['lambda_k1']).astype(jnp.float32))
           - jnp.exp(jnp.sum(p['lambda_q2'] * p['lambda_k2']).astype(jnp.float32)) + lam_init)
    k_all = jnp.concatenate([k_ctx, k_lat], axis=1)
    v_all = jnp.concatenate([v_ctx, v_heads(v_lat)], axis=1).astype(jnp.float32)

    def out(o):
        o = _rmsnorm(o, p['subln']) * (1.0 - lam_init)
        return o.reshape(o.shape[:2] + (DIFF_V,)).astype(h_lat.dtype) @ p['w_o']

    y_lat = out(_diff_attention(q_lat, k_all, v_all, lam))
    y_ctx = None
    if need_ctx:
        q_ctx = qk_heads(h_ctx @ p['w_qkv'][:, :DIFF_QK])
        y_ctx = out(_diff_attention(q_ctx, k_ctx, v_ctx.astype(jnp.float32), lam))
    return y_lat, y_ctx


def _expert_choice_moe(h, p):
    bn, n, _ = h.shape
    cap = CAPACITY_FACTOR * n // N_EXPERTS
    aff = jax.nn.softmax((h @ p['router']).astype(jnp.float32), axis=-1)
    gate, idx = lax.top_k(jnp.swapaxes(aff, 1, 2), cap)
    bidx = jnp.arange(bn)[:, None, None]
    xe = h[bidx, idx]
    hid = jax.nn.silu(jnp.einsum('becd,edf->becf', xe, p['exp_gate'])) * jnp.einsum('becd,edf->becf', xe, p['exp_up'])
    ye = jnp.einsum('becf,efd->becd', hid, p['exp_down']) * gate[..., None].astype(h.dtype)
    return jnp.zeros_like(h).at[bidx, idx].add(ye)


def setup_inputs(seed: int = 0) -> dict:
    key = jax.random.key(seed)
    keys = iter(jax.random.split(key, 48))
    D = D_MODEL

    def nrm(shape, scale):
        return scale * jax.random.normal(next(keys), shape, jnp.float32)

    def gain(n):
        return 1.0 + nrm((n,), 0.05)

    def experts():
        return (nrm((N_EXPERTS, D, EXPERT_FF), D ** -0.5), nrm((N_EXPERTS, D, EXPERT_FF), D ** -0.5),
                nrm((N_EXPERTS, EXPERT_FF, D), EXPERT_FF ** -0.5))

    a_log = jnp.log(jax.random.uniform(next(keys), (2, GDN_HEADS), jnp.float32, 1.0, 16.0))
    dt = jnp.exp(jax.random.uniform(next(keys), (2, GDN_HEADS), jnp.float32, math.log(1e-3), math.log(1e-1)))
    dt_bias = dt + jnp.log(-jnp.expm1(-dt))
    g0, u0, d0 = experts()
    g1, u1, d1 = experts()
    return {
        'x': nrm((BATCH, SEQ, D), 1.0),
        'c': nrm((BATCH, D), 1.0),
        'ctx': nrm((BATCH, CTX_LEN, D), 1.0),
        'c_ctx': nrm((D,), 1.0),
        'l0_mod_w': nrm((D, 6 * D), 0.5 * D ** -0.5),
        'l0_mod_b': nrm((6 * D,), 0.02),
        'l0_norm1': gain(D),
        'l0_w_in': nrm((D, IN_COLS), D ** -0.5),
        'l0_conv_a': nrm((CONV_W, CONV_DIM), CONV_W ** -0.5),
        'l0_conv_qkv': nrm((CONV_W, 2 * GDN_QK + GDN_V), CONV_W ** -0.5),
        'l0_a_log': a_log,
        'l0_dt_bias': dt_bias,
        'l0_gdn_norm': gain(GDN_DV),
        'l0_w_out': nrm((MIX_DIM, D), MIX_DIM ** -0.5),
        'l0_norm2': gain(D),
        'l0_router': nrm((D, N_EXPERTS), D ** -0.5),
        'l0_exp_gate': g0,
        'l0_exp_up': u0,
        'l0_exp_down': d0,
        'l1_mod_w': nrm((D, 6 * D), 0.5 * D ** -0.5),
        'l1_mod_b': nrm((6 * D,), 0.02),
        'l1_norm1': gain(D),
        'l1_w_qkv': nrm((D, 2 * DIFF_QK + DIFF_V), D ** -0.5),
        'l1_lambda_q1': nrm((DIFF_DH,), 0.1),
        'l1_lambda_k1': nrm((DIFF_DH,), 0.1),
        'l1_lambda_q2': nrm((DIFF_DH,), 0.1),
        'l1_lambda_k2': nrm((DIFF_DH,), 0.1),
        'l1_subln': gain(2 * DIFF_DH),
        'l1_w_o': nrm((DIFF_V, D), DIFF_V ** -0.5),
        'l1_norm2': gain(D),
        'l1_router': nrm((D, N_EXPERTS), D ** -0.5),
        'l1_exp_gate': g1,
        'l1_exp_up': u1,
        'l1_exp_down': d1,
        'final_norm': gain(D),
    }


def reference(x, c, ctx, c_ctx,
              l0_mod_w, l0_mod_b, l0_norm1, l0_w_in, l0_conv_a, l0_conv_qkv, l0_a_log, l0_dt_bias,
              l0_gdn_norm, l0_w_out, l0_norm2, l0_router, l0_exp_gate, l0_exp_up, l0_exp_down,
              l1_mod_w, l1_mod_b, l1_norm1, l1_w_qkv, l1_lambda_q1, l1_lambda_k1, l1_lambda_q2,
              l1_lambda_k2, l1_subln, l1_w_o, l1_norm2, l1_router, l1_exp_gate, l1_exp_up, l1_exp_down,
              final_norm):
    layers = (
        dict(mod_w=l0_mod_w, mod_b=l0_mod_b, norm1=l0_norm1, w_in=l0_w_in, conv_a=l0_conv_a,
             conv_qkv=l0_conv_qkv, a_log=l0_a_log, dt_bias=l0_dt_bias, gdn_norm=l0_gdn_norm,
             w_out=l0_w_out, norm2=l0_norm2, router=l0_router, exp_gate=l0_exp_gate,
             exp_up=l0_exp_up, exp_down=l0_exp_down),
        dict(mod_w=l1_mod_w, mod_b=l1_mod_b, norm1=l1_norm1, w_qkv=l1_w_qkv, lambda_q1=l1_lambda_q1,
             lambda_k1=l1_lambda_k1, lambda_q2=l1_lambda_q2, lambda_k2=l1_lambda_k2, subln=l1_subln,
             w_o=l1_w_o, norm2=l1_norm2, router=l1_router, exp_gate=l1_exp_gate,
             exp_up=l1_exp_up, exp_down=l1_exp_down),
    )
    for i in range(DEPTH):
        p = layers[i]
        last = i == DEPTH - 1
        m = _adaln(c, p['mod_w'], p['mod_b'])
        mc = _adaln(c_ctx[None], p['mod_w'], p['mod_b'])
        h = _modulate(x, p['norm1'], m[0], m[1])
        hc = _modulate(ctx, p['norm1'], mc[0], mc[1])
        if i % 2 == 0:
            s0 = jnp.zeros((x.shape[0], GDN_HEADS, GDN_DK, GDN_DV), jnp.float32)
            y_ctx, s_f, s_b = _conv_gdn_mixer(hc, p, s0, s0)
            y, _, _ = _conv_gdn_mixer(h, p, s_f, s_b)
        else:
            y, y_ctx = _diff_attn_mixer(h, hc, p, 0.8 - 0.6 * math.exp(-0.3 * i), not last)
        x = x + m[2] * y
        x = x + m[5] * _expert_choice_moe(_modulate(x, p['norm2'], m[3], m[4]), p)
        if not last:
            ctx = ctx + mc[2] * y_ctx
            ctx = ctx + mc[5] * _expert_choice_moe(_modulate(ctx, p['norm2'], mc[3], mc[4]), p)
    return _rmsnorm(x, final_norm)
```

```python
import functools
import math

import jax
import jax.numpy as jnp
import numpy as np
from jax import lax
from jax.experimental import pallas as pl
from jax.experimental.pallas import tpu as pltpu

F32 = jnp.float32
BF16 = jnp.bfloat16
I32 = jnp.int32

EPS = 1e-6
GRID_W = 64
ROPE_BASE = 10000.0
CONV_DIM = 512
GDN_HEADS = 4
GDN_DK = 128
GDN_CHUNK = 64
GDN_GROUP = 256
DIFF_HEADS = 8
DIFF_DH = 64
N_EXPERTS = 16
CAPACITY_FACTOR = 2
LANES = 128
SUBLANES = 8
TOPK_ROWS = 128
V7X_VMEM_LIMIT = 56 * 1024 * 1024
LOG2E = 1.4426950408889634


def _cparams(sem, vmem=None):
    return pltpu.CompilerParams(dimension_semantics=sem, vmem_limit_bytes=vmem or V7X_VMEM_LIMIT)


def _dot(a, b):
    return jnp.dot(a, b, preferred_element_type=F32)


def _dot_nt(a, b):
    return lax.dot_general(a, b, (((1,), (1,)), ((), ())), preferred_element_type=F32)


def _dot_tn(a, b):
    return lax.dot_general(a, b, (((0,), (0,)), ((), ())), preferred_element_type=F32)


def _split2(x):
    hi = x.astype(BF16)
    lo = (x - hi.astype(F32)).astype(BF16)
    return hi, lo


def _split3(x):
    hi = x.astype(BF16)
    r = x - hi.astype(F32)
    mid = r.astype(BF16)
    lo = (r - mid.astype(F32)).astype(BF16)
    return hi, mid, lo


def _dot_exact_lhs(m_bf16, x):
    hi, mid, lo = _split3(x)
    return _dot(m_bf16, hi) + _dot(m_bf16, mid) + _dot(m_bf16, lo)


def _mm3(a, b):
    ah, al = _split2(a)
    bh, bl = _split2(b)
    return _dot(ah, bh) + _dot(ah, bl) + _dot(al, bh)


def _silu(x):
    return x * jax.nn.sigmoid(x)


def _modulate_tile(x, nw, shift, scale):
    ms = jnp.mean(x * x, axis=-1, keepdims=True)
    return (x * lax.rsqrt(ms + EPS)) * nw * (1.0 + scale) + shift


def _adaln_kernel(c_ref, w_ref, b_ref, o_ref):
    s = _silu(c_ref[...])
    o_ref[...] = _dot(s.astype(BF16), w_ref[...].astype(BF16)) + b_ref[...]


def _adaln(cond8, w, b):
    d, n = w.shape
    tn = 1536
    return pl.pallas_call(
        _adaln_kernel,
        grid=(n // tn,),
        in_specs=[pl.BlockSpec((8, d), lambda j: (0, 0)),
                  pl.BlockSpec((d, tn), lambda j: (0, j)),
                  pl.BlockSpec((1, tn), lambda j: (0, j))],
        out_specs=pl.BlockSpec((8, tn), lambda j: (0, j)),
        out_shape=jax.ShapeDtypeStruct((8, n), F32),
        compiler_params=_cparams(("arbitrary",)),
        name="adaln",
    )(cond8, w, b.reshape(1, n))


def _in_proj_kernel(x_ref, sh_ref, sc_ref, nw_ref, w_ref, p_ref, g_ref, *, n_main, chunk):
    h = _modulate_tile(x_ref[0], nw_ref[...], sh_ref[0], sc_ref[0]).astype(BF16)
    for n0 in range(0, n_main, chunk):
        p_ref[0, :, n0:n0 + chunk] = _dot(h, w_ref[:, n0:n0 + chunk]).astype(BF16)
    g_ref[0] = _dot(h, w_ref[:, n_main:])


def _in_proj(x, shift, scale, nw, w_pad, n_main):
    b, l, d = x.shape
    tm = min(512, l)
    n_all = w_pad.shape[1]
    return pl.pallas_call(
        functools.partial(_in_proj_kernel, n_main=n_main, chunk=512),
        grid=(b, l // tm),
        in_specs=[pl.BlockSpec((1, tm, d), lambda i, j: (i, j, 0)),
                  pl.BlockSpec((1, 1, d), lambda i, j: (i, 0, 0)),
                  pl.BlockSpec((1, 1, d), lambda i, j: (i, 0, 0)),
                  pl.BlockSpec((1, d), lambda i, j: (0, 0)),
                  pl.BlockSpec((d, n_all), lambda i, j: (0, 0))],
        out_specs=[pl.BlockSpec((1, tm, n_main), lambda i, j: (i, j, 0)),
                   pl.BlockSpec((1, tm, LANES), lambda i, j: (i, j, 0))],
        out_shape=[jax.ShapeDtypeStruct((b, l, n_main), BF16),
                   jax.ShapeDtypeStruct((b, l, LANES), F32)],
        compiler_params=_cparams(("arbitrary", "arbitrary")),
        name="l0_in_proj",
    )(x, shift, scale, nw, w_pad)


def _conv_prep_kernel(p_ref, pp_ref, pn_ref, g_ref, ca_ref, cq_ref, al_ref, dt_ref,
                      yc_ref, qkv_ref, gp_ref, *, tm, halo):
    i = pl.program_id(1)
    has_prev = i > 0
    has_next = i < pl.num_programs(1) - 1
    row = lax.broadcasted_iota(I32, (tm, 1), 0)

    def conv3(cur, prev_row, next_row, w_ref, c0, c1):
        prev_row = jnp.where(has_prev, prev_row, 0.0)
        next_row = jnp.where(has_next, next_row, 0.0)
        up = jnp.where(row == 0, prev_row, pltpu.roll(cur, 1, axis=0))
        dn = jnp.where(row == tm - 1, next_row, pltpu.roll(cur, tm - 1, axis=0))
        return up * w_ref[0:1, c0:c1] + cur * w_ref[1:2, c0:c1] + dn * w_ref[2:3, c0:c1]

    def cols(ref, r0, r1, c0, c1):
        return ref[0, r0:r1, c0:c1].astype(F32)

    cd = CONV_DIM
    cur = cols(p_ref, 0, tm, 2 * cd, 3 * cd) * cols(p_ref, 0, tm, 0, cd)
    prv = cols(pp_ref, halo - 1, halo, 2 * cd, 3 * cd) * cols(pp_ref, halo - 1, halo, 0, cd)
    nxt = cols(pn_ref, 0, 1, 2 * cd, 3 * cd) * cols(pn_ref, 0, 1, 0, cd)
    yc = cols(p_ref, 0, tm, cd, 2 * cd) * conv3(cur, prv, nxt, ca_ref, 0, cd)
    yc_ref[0] = yc.astype(BF16)

    base = 3 * cd
    for part in range(3):
        c0 = base + part * cd
        cur = cols(p_ref, 0, tm, c0, c0 + cd)
        prv = cols(pp_ref, halo - 1, halo, c0, c0 + cd)
        nxt = cols(pn_ref, 0, 1, c0, c0 + cd)
        y = _silu(conv3(cur, prv, nxt, cq_ref, part * cd, (part + 1) * cd))
        if part < 2:
            outs = []
            for hd in range(GDN_HEADS):
                yh = y[:, hd * GDN_DK:(hd + 1) * GDN_DK]
                ss = jnp.sum(yh * yh, axis=-1, keepdims=True)
                yh = yh * lax.rsqrt(ss + EPS)
                if part == 0:
                    yh = yh * (GDN_DK ** -0.5)
                outs.append(yh)
            y = jnp.concatenate(outs, axis=1)
        qkv_ref[0, :, part * cd:(part + 1) * cd] = y.astype(BF16)

    g = g_ref[0]
    lane = lax.broadcasted_iota(I32, g.shape, 1)
    z = g + dt_ref[...]
    softplus = jnp.maximum(z, 0.0) + jnp.log1p(jnp.exp(-jnp.abs(z)))
    decay = -jnp.exp(al_ref[...]) * softplus
    gp_ref[0] = jnp.where(lane < 2 * GDN_HEADS, decay, jax.nn.sigmoid(g))


def _conv_prep(p, gates, conv_a, conv_qkv, alog_vec, dt_vec):
    b, l, n_main = p.shape
    tm = min(512, l)
    halo = 16
    nh = l // halo
    r = tm // halo
    return pl.pallas_call(
        functools.partial(_conv_prep_kernel, tm=tm, halo=halo),
        grid=(b, l // tm),
        in_specs=[pl.BlockSpec((1, tm, n_main), lambda i, j: (i, j, 0)),
                  pl.BlockSpec((1, halo, n_main), lambda i, j: (i, jnp.maximum(j * r - 1, 0), 0)),
                  pl.BlockSpec((1, halo, n_main), lambda i, j: (i, jnp.minimum((j + 1) * r, nh - 1), 0)),
                  pl.BlockSpec((1, tm, LANES), lambda i, j: (i, j, 0)),
                  pl.BlockSpec(conv_a.shape, lambda i, j: (0, 0)),
                  pl.BlockSpec(conv_qkv.shape, lambda i, j: (0, 0)),
                  pl.BlockSpec((1, LANES), lambda i, j: (0, 0)),
                  pl.BlockSpec((1, LANES), lambda i, j: (0, 0))],
        out_specs=[pl.BlockSpec((1, tm, CONV_DIM), lambda i, j: (i, j, 0)),
                   pl.BlockSpec((1, tm, 3 * CONV_DIM), lambda i, j: (i, j, 0)),
                   pl.BlockSpec((1, tm, LANES), lambda i, j: (i, j, 0))],
        out_shape=[jax.ShapeDtypeStruct((b, l, CONV_DIM), BF16),
                   jax.ShapeDtypeStruct((b, l, 3 * CONV_DIM), BF16),
                   jax.ShapeDtypeStruct((b, l, LANES), F32)],
        compiler_params=_cparams(("arbitrary", "arbitrary")),
        name="l0_conv_prep",
    )(p, p, p, gates, conv_a, conv_qkv, alog_vec, dt_vec)


def _gdn_kernel(qkv_ref, gp_ref, s0_ref, o_ref, sout_ref, state, *, rev):
    i = pl.program_id(1)
    n = GDN_GROUP
    c = GDN_CHUNK
    nc = n // c
    dirn = 1 if rev else 0

    @pl.when(i == 0)
    def _():
        state[...] = s0_ref[0]

    ri = lax.broadcasted_iota(I32, (n, n), 0)
    ci = lax.broadcasted_iota(I32, (n, n), 1)
    same = (ri // c) == (ci // c)
    if rev:
        incl = jnp.logical_and(same, ci >= ri)
        strict = jnp.logical_and(same, ci > ri)
    else:
        incl = jnp.logical_and(same, ci <= ri)
        strict = jnp.logical_and(same, ci < ri)
    eye = jnp.where(ri == ci, 1.0, 0.0).astype(F32)

    gp = gp_ref[0]
    gc_all = _dot_exact_lhs(jnp.where(incl, 1.0, 0.0).astype(BF16), gp)
    gt_all = _dot_exact_lhs(jnp.where(same, 1.0, 0.0).astype(BF16), gp)
    gc_rows = gc_all.T

    for hd in range(GDN_HEADS):
        gcol = dirn * GDN_HEADS + hd
        bcol = 2 * GDN_HEADS + gcol
        gc = gc_all[:, gcol:gcol + 1]
        gcr = gc_rows[gcol:gcol + 1, :]
        gt = gt_all[:, gcol:gcol + 1]
        beta = gp[:, bcol:bcol + 1]
        q16 = qkv_ref[0, :, hd * GDN_DK:(hd + 1) * GDN_DK]
        k16 = qkv_ref[0, :, (GDN_HEADS + hd) * GDN_DK:(GDN_HEADS + hd + 1) * GDN_DK]
        v16 = qkv_ref[0, :, (2 * GDN_HEADS + hd) * GDN_DK:(2 * GDN_HEADS + hd + 1) * GDN_DK]
        q = q16.astype(F32)
        k = k16.astype(F32)
        v = v16.astype(F32)

        decay = jnp.where(incl, jnp.exp(jnp.where(incl, gc - gcr, 0.0)), 0.0)
        kb = k * beta
        a = jnp.where(strict, _dot_nt(kb.astype(BF16), k16) * decay, 0.0)
        t = eye - a
        pw = _mm3(a, a)
        for step in range(5):
            t = t + _mm3(t, pw)
            if step < 4:
                pw = _mm3(pw, pw)

        ex = jnp.exp(gc)
        rhs = jnp.concatenate([(v * beta).astype(BF16), (kb * ex).astype(BF16)], axis=1)
        uw = _dot(t.astype(BF16), rhs)
        u = uw[:, :GDN_DK]
        w = uw[:, GDN_DK:].astype(BF16)
        intra = jnp.where(incl, _dot_nt(q16, k16) * decay, 0.0).astype(BF16)
        qd = (q * ex).astype(BF16)
        kd = (k * jnp.exp(gt - gc)).astype(BF16)

        s = state[hd]
        order = range(nc - 1, -1, -1) if rev else range(nc)
        for cc in order:
            r0 = cc * c
            sb = s.astype(BF16)
            vn = u[r0:r0 + c] - _dot(w[r0:r0 + c], sb)
            vn16 = vn.astype(BF16)
            o_c = _dot(qd[r0:r0 + c], sb) + _dot(intra[r0:r0 + c, r0:r0 + c], vn16)
            gl = jnp.exp(gt[r0:r0 + 1, :])
            s = s * gl + _dot_tn(kd[r0:r0 + c], vn16)
            o_ref[0, r0:r0 + c, hd * GDN_DK:(hd + 1) * GDN_DK] = o_c
        state[hd] = s

    @pl.when(i == pl.num_programs(1) - 1)
    def _():
        sout_ref[0] = state[...]


def _gdn_scan(qkv, gp, s0, rev):
    b, l, _ = qkv.shape
    n = GDN_GROUP
    g = l // n
    hdv = GDN_HEADS * GDN_DK
    if rev:
        idx = lambda i, j: (i, g - 1 - j, 0)
    else:
        idx = lambda i, j: (i, j, 0)
    return pl.pallas_call(
        functools.partial(_gdn_kernel, rev=rev),
        grid=(b, g),
        in_specs=[pl.BlockSpec((1, n, 3 * hdv), idx),
                  pl.BlockSpec((1, n, LANES), idx),
                  pl.BlockSpec((1, GDN_HEADS, GDN_DK, GDN_DK), lambda i, j: (i, 0, 0, 0))],
        out_specs=[pl.BlockSpec((1, n, hdv), idx),
                   pl.BlockSpec((1, GDN_HEADS, GDN_DK, GDN_DK), lambda i, j: (i, 0, 0, 0))],
        out_shape=[jax.ShapeDtypeStruct((b, l, hdv), F32),
                   jax.ShapeDtypeStruct((b, GDN_HEADS, GDN_DK, GDN_DK), F32)],
        scratch_shapes=[pltpu.VMEM((GDN_HEADS, GDN_DK, GDN_DK), F32)],
        compiler_params=_cparams(("arbitrary", "arbitrary")),
        name="l0_gdn_bwd" if rev else "l0_gdn_fwd",
    )(qkv, gp, s0)


def _gdn_out_kernel(of_ref, ob_ref, z_ref, yc_ref, x_ref, gate_ref, gn_ref, w_ref, o_ref):
    o = of_ref[0] + ob_ref[0]
    z = z_ref[0].astype(F32)
    parts = []
    for hd in range(GDN_HEADS):
        oh = o[:, hd * GDN_DK:(hd + 1) * GDN_DK]
        ms = jnp.mean(oh * oh, axis=-1, keepdims=True)
        parts.append(oh * lax.rsqrt(ms + EPS) * gn_ref[...] * _silu(z[:, hd * GDN_DK:(hd + 1) * GDN_DK]))
    yg = jnp.concatenate(parts, axis=1).astype(BF16)
    cd = CONV_DIM
    y = _dot(yc_ref[0], w_ref[0:cd, :]) + _dot(yg, w_ref[cd:, :])
    o_ref[0] = x_ref[0] + gate_ref[0] * y


def _gdn_out(o_f, o_b, p, yconv, x, gate, gn, w_out):
    b, l, d = x.shape
    tm = min(512, l)
    hdv = GDN_HEADS * GDN_DK
    zblk = (3 * CONV_DIM + 3 * hdv) // hdv
    row = lambda i, j: (i, j, 0)
    return pl.pallas_call(
        _gdn_out_kernel,
        grid=(b, l // tm),
        in_specs=[pl.BlockSpec((1, tm, hdv), row),
                  pl.BlockSpec((1, tm, hdv), row),
                  pl.BlockSpec((1, tm, hdv), lambda i, j: (i, j, zblk)),
                  pl.BlockSpec((1, tm, CONV_DIM), row),
                  pl.BlockSpec((1, tm, d), row),
                  pl.BlockSpec((1, 1, d), lambda i, j: (i, 0, 0)),
                  pl.BlockSpec((1, GDN_DK), lambda i, j: (0, 0)),
                  pl.BlockSpec(w_out.shape, lambda i, j: (0, 0))],
        out_specs=pl.BlockSpec((1, tm, d), row),
        out_shape=jax.ShapeDtypeStruct((b, l, d), F32),
        compiler_params=_cparams(("arbitrary", "arbitrary")),
        name="l0_out_proj",
    )(o_f, o_b, p, yconv, x, gate, gn, w_out)


def _router_kernel(x_ref, sh_ref, sc_ref, nw_ref, rt_ref, hm3_ref, aff_ref, *, tm):
    hm = _modulate_tile(x_ref[0], nw_ref[...], sh_ref[0], sc_ref[0])
    for cc in range(SUBLANES):
        hm3_ref[0, pl.ds(cc, tm, stride=SUBLANES), :] = hm[:, cc * LANES:(cc + 1) * LANES]
    rh, rl = _split2(rt_ref[...])
    hh, hl = _split2(hm)
    logits = _dot_nt(rh, hh) + _dot_nt(rh, hl) + _dot_nt(rl, hh)
    m = jnp.max(logits, axis=0, keepdims=True)
    e = jnp.exp(logits - m)
    aff_ref[0] = e / jnp.sum(e, axis=0, keepdims=True)


def _router(x, shift, scale, nw, router_t):
    b, l, d = x.shape
    tm = min(512, l)
    return pl.pallas_call(
        functools.partial(_router_kernel, tm=tm),
        grid=(b, l // tm),
        in_specs=[pl.BlockSpec((1, tm, d), lambda i, j: (i, j, 0)),
                  pl.BlockSpec((1, 1, d), lambda i, j: (i, 0, 0)),
                  pl.BlockSpec((1, 1, d), lambda i, j: (i, 0, 0)),
                  pl.BlockSpec((1, d), lambda i, j: (0, 0)),
                  pl.BlockSpec((N_EXPERTS, d), lambda i, j: (0, 0))],
        out_specs=[pl.BlockSpec((1, tm * SUBLANES, LANES), lambda i, j: (i, j, 0)),
                   pl.BlockSpec((1, N_EXPERTS, tm), lambda i, j: (i, 0, j))],
        out_shape=[jax.ShapeDtypeStruct((b, l * SUBLANES, LANES), F32),
                   jax.ShapeDtypeStruct((b, N_EXPERTS, l), F32)],
        compiler_params=_cparams(("arbitrary", "arbitrary")),
        name="moe_router",
    )(x, shift, scale, nw, router_t)


def _topk_kernel(aff_ref, idx_ref, gate_ref, *, cap, capp):
    r = TOPK_ROWS
    a3 = aff_ref[0]
    bits3 = pltpu.bitcast(a3, I32)

    def count_ge(cand):
        hit = jnp.where(bits3 >= cand, 1.0, 0.0)
        return jnp.sum(jnp.sum(hit, axis=1, keepdims=True), axis=2, keepdims=True)

    thr = jnp.zeros((N_EXPERTS, 1, 1), I32)
    for bit in range(30, -1, -1):
        cand = thr | (1 << bit)
        thr = jnp.where(count_ge(cand) >= cap, cand, thr)

    li = lax.broadcasted_iota(I32, (LANES, LANES), 0)
    lj = lax.broadcasted_iota(I32, (LANES, LANES), 1)
    tri_incl = jnp.where(li <= lj, 1.0, 0.0).astype(BF16)
    tri_strict_rows = jnp.where(lj < li, 1.0, 0.0).astype(BF16)
    ones8 = jnp.ones((SUBLANES, LANES), BF16)
    lane_row = lax.broadcasted_iota(I32, (capp, LANES), 1).astype(F32)
    slot = lax.broadcasted_iota(I32, (capp, 1), 0).astype(F32)

    for ex in range(N_EXPERTS):
        a = aff_ref[0, ex]
        bits = pltpu.bitcast(a, I32)
        th = thr[ex]
        gt = bits > th
        eq = bits == th
        n_gt = jnp.sum(jnp.sum(jnp.where(gt, 1.0, 0.0), axis=0, keepdims=True), axis=1, keepdims=True)
        need = cap - n_gt
        eq16 = jnp.where(eq, 1.0, 0.0).astype(BF16)
        eq_lc = _dot(eq16, tri_incl)
        eq_cnt = jnp.broadcast_to(eq_lc[:, LANES - 1:LANES], (r, LANES)).astype(BF16)
        eq_off = _dot(tri_strict_rows, eq_cnt)
        take_eq = jnp.logical_and(eq, (eq_off + eq_lc) <= need)
        sel16 = jnp.where(jnp.logical_or(gt, take_eq), 1.0, 0.0).astype(BF16)

        lc = _dot(sel16, tri_incl)
        cnt_row = _dot_nt(ones8, sel16)
        off_incl = _dot(cnt_row.astype(BF16), tri_incl)
        off_excl = off_incl - cnt_row
        jrow = jnp.sum(jnp.where(off_incl[0:1, :] <= slot, 1.0, 0.0), axis=1, keepdims=True)
        onehot = lane_row == jrow
        onehot16 = jnp.where(onehot, 1.0, 0.0).astype(BF16)
        lc_rows = _dot(onehot16, lc.astype(BF16))
        off_s = jnp.sum(jnp.where(onehot, off_excl[0:1, :], 0.0), axis=1, keepdims=True)
        rank = slot - off_s
        pos = jnp.sum(jnp.where(lc_rows <= rank, 1.0, 0.0), axis=1, keepdims=True)
        tok = jrow * float(LANES) + pos
        ah, am, al = _split3(a)
        a_rows = _dot(onehot16, ah) + _dot(onehot16, am) + _dot(onehot16, al)
        gate = jnp.sum(jnp.where(lane_row == pos, a_rows, 0.0), axis=1, keepdims=True)
        tok_t = jnp.broadcast_to(tok, (capp, LANES)).T
        gate_t = jnp.broadcast_to(gate, (capp, LANES)).T
        idx_ref[0, ex] = tok_t[0:1, :].astype(I32)
        gate_ref[0, ex] = gate_t[0:1, :]


def _topk(aff_t, cap):
    b, e, n = aff_t.shape
    npad = TOPK_ROWS * LANES
    capp = max(cap, LANES)
    a = jnp.pad(aff_t, ((0, 0), (0, 0), (0, npad - n)), constant_values=-1.0).reshape(b, e, TOPK_ROWS, LANES)
    idx, gate = pl.pallas_call(
        functools.partial(_topk_kernel, cap=cap, capp=capp),
        grid=(b,),
        in_specs=[pl.BlockSpec((1, e, TOPK_ROWS, LANES), lambda i: (i, 0, 0, 0))],
        out_specs=[pl.BlockSpec((1, e, 1, capp), lambda i: (i, 0, 0, 0)),
                   pl.BlockSpec((1, e, 1, capp), lambda i: (i, 0, 0, 0))],
        out_shape=[jax.ShapeDtypeStruct((b, e, 1, capp), I32),
                   jax.ShapeDtypeStruct((b, e, 1, capp), F32)],
        compiler_params=_cparams(("arbitrary",)),
        name="moe_topk",
    )(a)
    return idx[:, :, 0, :cap], gate[:, :, 0, :cap]


def _gather_kernel(idx_ref, hm3_hbm, xe_ref, stage, sem, *, cap):
    b = pl.program_id(0)

    def row_copy(s):
        t = idx_ref[0, 0, s]
        src = hm3_hbm.at[b, pl.ds(pl.multiple_of(t * SUBLANES, SUBLANES), SUBLANES), :]
        dst = stage.at[pl.ds(pl.multiple_of(s * SUBLANES, SUBLANES), SUBLANES), :]
        return pltpu.make_async_copy(src, dst, sem)

    def issue(g, carry):
        for u in range(SUBLANES):
            row_copy(g * SUBLANES + u).start()
        return carry

    lax.fori_loop(0, cap // SUBLANES, issue, 0)
    pltpu.make_async_copy(hm3_hbm.at[b, pl.ds(0, cap * SUBLANES), :], stage, sem).wait()
    for cc in range(SUBLANES):
        xe_ref[0, 0, :, cc * LANES:(cc + 1) * LANES] = stage[pl.ds(cc, cap, stride=SUBLANES), :].astype(BF16)


def _gather(idx, hm3, d):
    b, e, cap = idx.shape
    return pl.pallas_call(
        functools.partial(_gather_kernel, cap=cap),
        grid=(b, e),
        in_specs=[pl.BlockSpec((1, 1, cap), lambda i, j: (i * e + j, 0, 0), memory_space=pltpu.SMEM),
                  pl.BlockSpec(memory_space=pl.ANY)],
        out_specs=pl.BlockSpec((1, 1, cap, d), lambda i, j: (i, j, 0, 0)),
        out_shape=jax.ShapeDtypeStruct((b, e, cap, d), BF16),
        scratch_shapes=[pltpu.VMEM((cap * SUBLANES, LANES), F32), pltpu.SemaphoreType.DMA(())],
        compiler_params=_cparams(("arbitrary", "arbitrary")),
        name="moe_gather",
    )(idx.reshape(b * e, 1, cap), hm3)


def _ffn_kernel(x_ref, wg_ref, wu_ref, wd_ref, y3_ref, acc, *, cap):
    f = pl.program_id(2)
    x = x_ref[0, 0]
    g = _dot(x, wg_ref[0].astype(BF16))
    u = _dot(x, wu_ref[0].astype(BF16))
    h = (_silu(g) * u).astype(BF16)
    part = _dot(h, wd_ref[0].astype(BF16))

    @pl.when(f == 0)
    def _():
        acc[...] = part

    @pl.when(f > 0)
    def _():
        acc[...] += part

    @pl.when(f == pl.num_programs(2) - 1)
    def _():
        for cc in range(SUBLANES):
            y3_ref[0, 0, pl.ds(cc, cap, stride=SUBLANES), :] = acc[:, cc * LANES:(cc + 1) * LANES]


def _ffn(xe, wg, wu, wd):
    b, e, cap, d = xe.shape
    ff = wg.shape[2]
    tf = 512
    return pl.pallas_call(
        functools.partial(_ffn_kernel, cap=cap),
        grid=(b, e, ff // tf),
        in_specs=[pl.BlockSpec((1, 1, cap, d), lambda i, j, k: (i, j, 0, 0)),
                  pl.BlockSpec((1, d, tf), lambda i, j, k: (j, 0, k)),
                  pl.BlockSpec((1, d, tf), lambda i, j, k: (j, 0, k)),
                  pl.BlockSpec((1, tf, d), lambda i, j, k: (j, k, 0))],
        out_specs=pl.BlockSpec((1, 1, cap * SUBLANES, LANES), lambda i, j, k: (i, j, 0, 0)),
        out_shape=jax.ShapeDtypeStruct((b, e, cap * SUBLANES, LANES), F32),
        scratch_shapes=[pltpu.VMEM((cap, d), F32)],
        compiler_params=_cparams(("arbitrary", "arbitrary", "arbitrary")),
        name="moe_ffn",
    )(xe, wg, wu, wd)


def _combine_kernel(idx_ref, gate_ref, y3_ref, out_hbm, acc, sem, *, cap, unroll):
    b = pl.program_id(0)
    e = pl.program_id(1)

    @pl.when(e == 0)
    def _():
        acc[...] = jnp.zeros_like(acc)

    def body(g, carry):
        base = g * unroll
        rows = []
        for u in range(unroll):
            s = base + u
            t = pl.multiple_of(idx_ref[0, 0, s] * SUBLANES, SUBLANES)
            y = y3_ref[0, 0, pl.ds(pl.multiple_of(s * SUBLANES, SUBLANES), SUBLANES), :]
            rows.append((t, acc[pl.ds(t, SUBLANES), :] + gate_ref[0, 0, s] * y))
        for t, val in rows:
            acc[pl.ds(t, SUBLANES), :] = val
        return carry

    lax.fori_loop(0, cap // unroll, body, 0)

    @pl.when(e == pl.num_programs(1) - 1)
    def _():
        cp = pltpu.make_async_copy(acc, out_hbm.at[b], sem)
        cp.start()
        cp.wait()


def _combine(idx, gate, y3, l):
    b, e, cap = idx.shape
    return pl.pallas_call(
        functools.partial(_combine_kernel, cap=cap, unroll=4),
        grid=(b, e),
        in_specs=[pl.BlockSpec((1, 1, cap), lambda i, j: (i * e + j, 0, 0), memory_space=pltpu.SMEM),
                  pl.BlockSpec((1, 1, cap), lambda i, j: (i * e + j, 0, 0), memory_space=pltpu.SMEM),
                  pl.BlockSpec((1, 1, cap * SUBLANES, LANES), lambda i, j: (i, j, 0, 0))],
        out_specs=pl.BlockSpec(memory_space=pl.ANY),
        out_shape=jax.ShapeDtypeStruct((b, l * SUBLANES, LANES), F32),
        scratch_shapes=[pltpu.VMEM((l * SUBLANES, LANES), F32), pltpu.SemaphoreType.DMA(())],
        compiler_params=_cparams(("arbitrary", "arbitrary")),
        name="moe_combine",
    )(idx.reshape(b * e, 1, cap), gate.reshape(b * e, 1, cap), y3)


def _moe_resid_kernel(x_ref, acc3_ref, gate_ref, fw_ref, o_ref, *, tm, final):
    parts = [acc3_ref[0, pl.ds(cc, tm, stride=SUBLANES), :] for cc in range(SUBLANES)]
    y = x_ref[0] + gate_ref[0] * jnp.concatenate(parts, axis=1)
    if final:
        ms = jnp.mean(y * y, axis=-1, keepdims=True)
        y = y * lax.rsqrt(ms + EPS) * fw_ref[...]
    o_ref[0] = y


def _moe_resid(x, acc3, gate, final_w, final):
    b, l, d = x.shape
    tm = min(512, l)
    return pl.pallas_call(
        functools.partial(_moe_resid_kernel, tm=tm, final=final),
        grid=(b, l // tm),
        in_specs=[pl.BlockSpec((1, tm, d), lambda i, j: (i, j, 0)),
                  pl.BlockSpec((1, tm * SUBLANES, LANES), lambda i, j: (i, j, 0)),
                  pl.BlockSpec((1, 1, d), lambda i, j: (i, 0, 0)),
                  pl.BlockSpec((1, d), lambda i, j: (0, 0))],
        out_specs=pl.BlockSpec((1, tm, d), lambda i, j: (i, j, 0)),
        out_shape=jax.ShapeDtypeStruct((b, l, d), F32),
        compiler_params=_cparams(("arbitrary", "arbitrary")),
        name="moe_resid",
    )(x, acc3, gate, final_w)


def _moe(x, shift, scale, gate, nw, router_t, wg, wu, wd, final_w, final):
    b, l, d = x.shape
    cap = CAPACITY_FACTOR * l // N_EXPERTS
    hm3, aff_t = _router(x, shift, scale, nw, router_t)
    idx, gates = _topk(aff_t, cap)
    xe = _gather(idx, hm3, d)
    y3 = _ffn(xe, wg, wu, wd)
    acc3 = _combine(idx, gates, y3, l)
    return _moe_resid(x, acc3, gate, final_w, final)


def _qkv_kernel(x_ref, sh_ref, sc_ref, nw_ref, w_ref, cos_ref, sin_ref, q_ref, k_ref, v_ref, *, rope, qscale):
    h = _modulate_tile(x_ref[0], nw_ref[...], sh_ref[0], sc_ref[0]).astype(BF16)
    dq = q_ref.shape[2]
    chunk = 512
    rep = chunk // LANES
    if rope:
        cos = jnp.concatenate([cos_ref[...]] * rep, axis=1)
        sin = jnp.concatenate([sin_ref[...]] * rep, axis=1)
        lane = lax.broadcasted_iota(I32, (x_ref.shape[1], chunk), 1)
        first = (lane % 32) < 16
    for which, ref in enumerate((q_ref, k_ref, v_ref)):
        for c0 in range(0, dq, chunk):
            y = _dot(h, w_ref[:, which * dq + c0:which * dq + c0 + chunk])
            if rope and which < 2:
                partner = jnp.where(first, pltpu.roll(y, chunk - 16, axis=1), pltpu.roll(y, 16, axis=1))
                y = y * cos + partner * sin
            if which == 0:
                y = y * qscale
            ref[0, :, c0:c0 + chunk] = y.astype(BF16)


def _qkv_proj(x, shift, scale, nw, w, cos_t, sin_t, rope):
    b, l, d = x.shape
    tm = min(512, l)
    dq = w.shape[1] // 3
    qscale = (DIFF_DH ** -0.5) * LOG2E
    row = lambda i, j: (i, j, 0)
    out = jax.ShapeDtypeStruct((b, l, dq), BF16)
    return pl.pallas_call(
        functools.partial(_qkv_kernel, rope=rope, qscale=qscale),
        grid=(b, l // tm),
        in_specs=[pl.BlockSpec((1, tm, d), row),
                  pl.BlockSpec((1, 1, d), lambda i, j: (i, 0, 0)),
                  pl.BlockSpec((1, 1, d), lambda i, j: (i, 0, 0)),
                  pl.BlockSpec((1, d), lambda i, j: (0, 0)),
                  pl.BlockSpec(w.shape, lambda i, j: (0, 0)),
                  pl.BlockSpec((tm, LANES), lambda i, j: (j, 0)),
                  pl.BlockSpec((tm, LANES), lambda i, j: (j, 0))],
        out_specs=[pl.BlockSpec((1, tm, dq), row)] * 3,
        out_shape=[out, out, out],
        compiler_params=_cparams(("arbitrary", "arbitrary")),
        name="l1_qkv_rope" if rope else "l1_qkv_ctx",
    )(x, shift, scale, nw, w, cos_t, sin_t)


def _rope_tables(l):
    t = np.arange(l)
    n_freq = DIFF_DH // 4
    inv_freq = ROPE_BASE ** (-np.arange(n_freq, dtype=np.float32) / n_freq)
    ang_row = (t // GRID_W).astype(np.float32)[:, None] * inv_freq
    ang_col = (t % GRID_W).astype(np.float32)[:, None] * inv_freq
    ang = np.concatenate([ang_row, ang_row, ang_col, ang_col], axis=1)
    sign = np.concatenate([-np.ones(n_freq), np.ones(n_freq)] * 2).astype(np.float32)
    cos = np.cos(ang).astype(np.float32)
    sin = (np.sin(ang) * sign).astype(np.float32)
    return jnp.asarray(np.concatenate([cos, cos], axis=1)), jnp.asarray(np.concatenate([sin, sin], axis=1))


def _attn_kernel(q_ref, k_ref, v_ref, lam_ref, sub_ref, o_ref, *, tk, lam_init):
    tq = q_ref.shape[1]
    lk = k_ref.shape[1]
    dv = v_ref.shape[2]
    q = q_ref[0]
    lane = lax.broadcasted_iota(I32, q.shape, 1)
    zero = jnp.zeros_like(q)
    q1 = jnp.where(lane < DIFF_DH, q, zero)
    q2 = jnp.where(lane >= DIFF_DH, q, zero)

    def update(s, v, m, l, acc):
        mn = jnp.maximum(m, jnp.max(s, axis=1, keepdims=True))
        alpha = jnp.exp2(m - mn)
        p = jnp.exp2(s - mn)
        l = alpha * l + jnp.sum(p, axis=1, keepdims=True)
        acc = alpha * acc + _dot(p.astype(BF16), v)
        return mn, l, acc

    def body(j, carry):
        m1, l1, a1, m2, l2, a2 = carry
        r0 = pl.multiple_of(j * tk, tk)
        k = k_ref[0, pl.ds(r0, tk), :]
        v = v_ref[0, pl.ds(r0, tk), :]
        m1, l1, a1 = update(_dot_nt(q1, k), v, m1, l1, a1)
        m2, l2, a2 = update(_dot_nt(q2, k), v, m2, l2, a2)
        return m1, l1, a1, m2, l2, a2

    m0 = jnp.full((tq, 1), -1e30, F32)
    l0 = jnp.zeros((tq, 1), F32)
    a0 = jnp.zeros((tq, dv), F32)
    m1, l1, a1, m2, l2, a2 = lax.fori_loop(0, lk // tk, body, (m0, l0, a0, m0, l0, a0))

    lp = lam_ref[...]
    lam = (jnp.exp(jnp.sum(lp[0:1] * lp[1:2], axis=1, keepdims=True))
           - jnp.exp(jnp.sum(lp[2:3] * lp[3:4], axis=1, keepdims=True)) + lam_init)
    o = a1 / l1 - lam * (a2 / l2)
    ms = jnp.mean(o * o, axis=-1, keepdims=True)
    o = o * lax.rsqrt(ms + EPS) * sub_ref[...] * (1.0 - lam_init)
    o_ref[0] = o.astype(BF16)


def _attention(q, k_all, v_all, lam_params, subln, lam_init):
    b, lq, dq = q.shape
    lk = k_all.shape[1]
    hd = 2 * DIFF_DH
    tq = min(256, lq)
    tk = 768 if lk % 768 == 0 else 256
    return pl.pallas_call(
        functools.partial(_attn_kernel, tk=tk, lam_init=lam_init),
        grid=(b, DIFF_HEADS, lq // tq),
        in_specs=[pl.BlockSpec((1, tq, hd), lambda i, h, j: (i, j, h)),
                  pl.BlockSpec((1, lk, hd), lambda i, h, j: (i, 0, h)),
                  pl.BlockSpec((1, lk, hd), lambda i, h, j: (i, 0, h)),
                  pl.BlockSpec(lam_params.shape, lambda i, h, j: (0, 0)),
                  pl.BlockSpec((1, hd), lambda i, h, j: (0, 0))],
        out_specs=pl.BlockSpec((1, tq, hd), lambda i, h, j: (i, j, h)),
        out_shape=jax.ShapeDtypeStruct((b, lq, dq), BF16),
        compiler_params=_cparams(("arbitrary", "arbitrary", "arbitrary")),
        name="l1_diff_attn",
    )(q, k_all, v_all, lam_params, subln)


def _resid_matmul_kernel(a_ref, w_ref, x_ref, gate_ref, o_ref):
    o_ref[0] = x_ref[0] + gate_ref[0] * _dot(a_ref[0], w_ref[...])


def _resid_matmul(a, w, x, gate):
    b, l, d = x.shape
    tm = min(512, l)
    row = lambda i, j: (i, j, 0)
    return pl.pallas_call(
        _resid_matmul_kernel,
        grid=(b, l // tm),
        in_specs=[pl.BlockSpec((1, tm, a.shape[2]), row),
                  pl.BlockSpec(w.shape, lambda i, j: (0, 0)),
                  pl.BlockSpec((1, tm, d), row),
                  pl.BlockSpec((1, 1, d), lambda i, j: (i, 0, 0))],
        out_specs=pl.BlockSpec((1, tm, d), row),
        out_shape=jax.ShapeDtypeStruct((b, l, d), F32),
        compiler_params=_cparams(("arbitrary", "arbitrary")),
        name="l1_out_proj",
    )(a, w, x, gate)


def _mod_vectors(c, c_ctx, w, b):
    bn, d = c.shape
    cond = jnp.concatenate([c, c_ctx[None], jnp.zeros((8 - bn - 1, d), F32)], axis=0)
    m = _adaln(cond, w, b)
    lat = [m[:bn, i * d:(i + 1) * d][:, None, :] for i in range(6)]
    ctx = [jnp.broadcast_to(m[bn:bn + 1, i * d:(i + 1) * d][:, None, :], (bn, 1, d)) for i in range(6)]
    return lat, ctx


def _row(v):
    return v.reshape(1, -1)


def kernel(x, c, ctx, c_ctx, l0_mod_w, l0_mod_b, l0_norm1, l0_w_in, l0_conv_a, l0_conv_qkv, l0_a_log, l0_dt_bias, l0_gdn_norm, l0_w_out, l0_norm2, l0_router, l0_exp_gate, l0_exp_up, l0_exp_down, l1_mod_w, l1_mod_b, l1_norm1, l1_w_qkv, l1_lambda_q1, l1_lambda_k1, l1_lambda_q2, l1_lambda_k2, l1_subln, l1_w_o, l1_norm2, l1_router, l1_exp_gate, l1_exp_up, l1_exp_down, final_norm):
    bn, seq, d = x.shape
    n_main = 3 * CONV_DIM + 4 * GDN_HEADS * GDN_DK
    n_gate = 4 * GDN_HEADS

    m, mc = _mod_vectors(c, c_ctx, l0_mod_w, l0_mod_b)
    w_in = jnp.concatenate([l0_w_in, jnp.zeros((d, LANES - n_gate), F32)], axis=1).astype(BF16)
    w_out = l0_w_out.astype(BF16)
    pad = jnp.zeros((LANES - 2 * GDN_HEADS,), F32)
    alog_vec = jnp.concatenate([l0_a_log.reshape(-1), pad]).reshape(1, LANES)
    dt_vec = jnp.concatenate([l0_dt_bias.reshape(-1), pad]).reshape(1, LANES)
    router0_t = l0_router.T

    def mixer0(tok, mod, s0_f, s0_b):
        p, gates = _in_proj(tok, mod[0], mod[1], _row(l0_norm1), w_in, n_main)
        yconv, qkv, gp = _conv_prep(p, gates, l0_conv_a, l0_conv_qkv, alog_vec, dt_vec)
        o_f, s_f = _gdn_scan(qkv, gp, s0_f, rev=False)
        o_b, s_b = _gdn_scan(qkv, gp, s0_b, rev=True)
        out = _gdn_out(o_f, o_b, p, yconv, tok, mod[2], _row(l0_gdn_norm), w_out)
        return out, s_f, s_b

    s0 = jnp.zeros((bn, GDN_HEADS, GDN_DK, GDN_DK), F32)
    ctx1, s_f, s_b = mixer0(ctx, mc, s0, s0)
    x1, _, _ = mixer0(x, m, s_f, s_b)
    moe0 = functools.partial(_moe, nw=_row(l0_norm2), router_t=router0_t, wg=l0_exp_gate, wu=l0_exp_up,
                             wd=l0_exp_down, final_w=_row(final_norm), final=False)
    x2 = moe0(x1, m[3], m[4], m[5])
    ctx2 = moe0(ctx1, mc[3], mc[4], mc[5])

    m, mc = _mod_vectors(c, c_ctx, l1_mod_w, l1_mod_b)
    lam_init = 0.8 - 0.6 * math.exp(-0.3 * 1)
    w_qkv = l1_w_qkv.astype(BF16)
    cos_t, sin_t = _rope_tables(seq)
    q, k_lat, v_lat = _qkv_proj(x2, m[0], m[1], _row(l1_norm1), w_qkv, cos_t, sin_t, rope=True)
    lc = ctx2.shape[1]
    _, k_ctx, v_ctx = _qkv_proj(ctx2, mc[0], mc[1], _row(l1_norm1), w_qkv, cos_t[:lc], sin_t[:lc], rope=False)
    k_all = jnp.concatenate([k_ctx, k_lat], axis=1)
    v_all = jnp.concatenate([v_ctx, v_lat], axis=1)
    lam_params = jnp.stack([l1_lambda_q1, l1_lambda_k1, l1_lambda_q2, l1_lambda_k2])
    o = _attention(q, k_all, v_all, lam_params, _row(l1_subln), lam_init)
    x3 = _resid_matmul(o, l1_w_o.astype(BF16), x2, m[2])
    return _moe(x3, m[3], m[4], m[5], _row(l1_norm2), l1_router.T, l1_exp_gate, l1_exp_up, l1_exp_down,
                _row(final_norm), True)
```

```python
import functools
import math

import jax
import jax.numpy as jnp
import numpy as np
from jax import lax
from jax.experimental import pallas as pl
from jax.experimental.pallas import tpu as pltpu

F32 = jnp.float32
BF16 = jnp.bfloat16
I32 = jnp.int32

EPS = 1e-6
GRID_W = 64
ROPE_BASE = 10000.0
CONV_DIM = 512
GDN_HEADS = 4
GDN_DK = 128
GDN_CHUNK = 64
GDN_GROUP = 256
DIFF_HEADS = 8
DIFF_DH = 64
N_EXPERTS = 16
CAPACITY_FACTOR = 2
LANES = 128
SUBLANES = 8
TOPK_ROWS = 128
V7X_VMEM_LIMIT = 56 * 1024 * 1024
LOG2E = 1.4426950408889634


def _cparams(sem, vmem=None):
    return pltpu.CompilerParams(dimension_semantics=sem, vmem_limit_bytes=vmem or V7X_VMEM_LIMIT)


def _dot(a, b):
    return jnp.dot(a, b, preferred_element_type=F32)


def _dot_nt(a, b):
    return lax.dot_general(a, b, (((1,), (1,)), ((), ())), preferred_element_type=F32)


def _dot_tn(a, b):
    return lax.dot_general(a, b, (((0,), (0,)), ((), ())), preferred_element_type=F32)


def _split2(x):
    hi = x.astype(BF16)
    lo = (x - hi.astype(F32)).astype(BF16)
    return hi, lo


def _split3(x):
    hi = x.astype(BF16)
    r = x - hi.astype(F32)
    mid = r.astype(BF16)
    lo = (r - mid.astype(F32)).astype(BF16)
    return hi, mid, lo


def _dot_exact_lhs(m_bf16, x):
    hi, mid, lo = _split3(x)
    return _dot(m_bf16, hi) + _dot(m_bf16, mid) + _dot(m_bf16, lo)


def _silu(x):
    return x * jax.nn.sigmoid(x)


def _modulate_tile(x, nw, shift, scale):
    ms = jnp.mean(x * x, axis=-1, keepdims=True)
    return (x * lax.rsqrt(ms + EPS)) * nw * (1.0 + scale) + shift


def _adaln_kernel(c_ref, w_ref, b_ref, o_ref):
    s = _silu(c_ref[...])
    o_ref[...] = _dot(s.astype(BF16), w_ref[...].astype(BF16)) + b_ref[...]


def _adaln(cond8, w, b):
    d, n = w.shape
    tn = 1536
    return pl.pallas_call(
        _adaln_kernel,
        grid=(n // tn,),
        in_specs=[pl.BlockSpec((8, d), lambda j: (0, 0)),
                  pl.BlockSpec((d, tn), lambda j: (0, j)),
                  pl.BlockSpec((1, tn), lambda j: (0, j))],
        out_specs=pl.BlockSpec((8, tn), lambda j: (0, j)),
        out_shape=jax.ShapeDtypeStruct((8, n), F32),
        compiler_params=_cparams(("arbitrary",)),
        name="adaln",
    )(cond8, w, b.reshape(1, n))


def _in_proj_kernel(x_ref, sh_ref, sc_ref, nw_ref, w_ref, p_ref, g_ref, *, n_main, chunk):
    h = _modulate_tile(x_ref[0], nw_ref[...], sh_ref[0], sc_ref[0]).astype(BF16)
    for n0 in range(0, n_main, chunk):
        p_ref[0, :, n0:n0 + chunk] = _dot(h, w_ref[:, n0:n0 + chunk]).astype(BF16)
    g_ref[0] = _dot(h, w_ref[:, n_main:])


def _in_proj(x, shift, scale, nw, w_pad, n_main):
    b, l, d = x.shape
    tm = min(512, l)
    n_all = w_pad.shape[1]
    return pl.pallas_call(
        functools.partial(_in_proj_kernel, n_main=n_main, chunk=512),
        grid=(b, l // tm),
        in_specs=[pl.BlockSpec((1, tm, d), lambda i, j: (i, j, 0)),
                  pl.BlockSpec((1, 1, d), lambda i, j: (i, 0, 0)),
                  pl.BlockSpec((1, 1, d), lambda i, j: (i, 0, 0)),
                  pl.BlockSpec((1, d), lambda i, j: (0, 0)),
                  pl.BlockSpec((d, n_all), lambda i, j: (0, 0))],
        out_specs=[pl.BlockSpec((1, tm, n_main), lambda i, j: (i, j, 0)),
                   pl.BlockSpec((1, tm, LANES), lambda i, j: (i, j, 0))],
        out_shape=[jax.ShapeDtypeStruct((b, l, n_main), BF16),
                   jax.ShapeDtypeStruct((b, l, LANES), F32)],
        compiler_params=_cparams(("arbitrary", "arbitrary")),
        name="l0_in_proj",
    )(x, shift, scale, nw, w_pad)


def _conv_prep_kernel(p_ref, pp_ref, pn_ref, g_ref, ca_ref, cq_ref, al_ref, dt_ref,
                      yc_ref, qkv_ref, gp_ref, *, tm, halo):
    i = pl.program_id(1)
    has_prev = i > 0
    has_next = i < pl.num_programs(1) - 1
    row = lax.broadcasted_iota(I32, (tm, 1), 0)

    def conv3(cur, prev_row, next_row, w_ref, c0, c1):
        prev_row = jnp.where(has_prev, prev_row, 0.0)
        next_row = jnp.where(has_next, next_row, 0.0)
        up = jnp.where(row == 0, prev_row, pltpu.roll(cur, 1, axis=0))
        dn = jnp.where(row == tm - 1, next_row, pltpu.roll(cur, tm - 1, axis=0))
        return up * w_ref[0:1, c0:c1] + cur * w_ref[1:2, c0:c1] + dn * w_ref[2:3, c0:c1]

    def cols(ref, r0, r1, c0, c1):
        return ref[0, r0:r1, c0:c1].astype(F32)

    cd = CONV_DIM
    cur = cols(p_ref, 0, tm, 2 * cd, 3 * cd) * cols(p_ref, 0, tm, 0, cd)
    prv = cols(pp_ref, halo - 1, halo, 2 * cd, 3 * cd) * cols(pp_ref, halo - 1, halo, 0, cd)
    nxt = cols(pn_ref, 0, 1, 2 * cd, 3 * cd) * cols(pn_ref, 0, 1, 0, cd)
    yc = cols(p_ref, 0, tm, cd, 2 * cd) * conv3(cur, prv, nxt, ca_ref, 0, cd)
    yc_ref[0] = yc.astype(BF16)

    base = 3 * cd
    for part in range(3):
        c0 = base + part * cd
        cur = cols(p_ref, 0, tm, c0, c0 + cd)
        prv = cols(pp_ref, halo - 1, halo, c0, c0 + cd)
        nxt = cols(pn_ref, 0, 1, c0, c0 + cd)
        y = _silu(conv3(cur, prv, nxt, cq_ref, part * cd, (part + 1) * cd))
        if part < 2:
            outs = []
            for hd in range(GDN_HEADS):
                yh = y[:, hd * GDN_DK:(hd + 1) * GDN_DK]
                ss = jnp.sum(yh * yh, axis=-1, keepdims=True)
                yh = yh * lax.rsqrt(ss + EPS)
                if part == 0:
                    yh = yh * (GDN_DK ** -0.5)
                outs.append(yh)
            y = jnp.concatenate(outs, axis=1)
        qkv_ref[0, :, part * cd:(part + 1) * cd] = y.astype(BF16)

    g = g_ref[0]
    lane = lax.broadcasted_iota(I32, g.shape, 1)
    z = g + dt_ref[...]
    softplus = jnp.maximum(z, 0.0) + jnp.log1p(jnp.exp(-jnp.abs(z)))
    decay = -jnp.exp(al_ref[...]) * softplus
    gp_ref[0] = jnp.where(lane < 2 * GDN_HEADS, decay, jax.nn.sigmoid(g))


def _conv_prep(p, gates, conv_a, conv_qkv, alog_vec, dt_vec):
    b, l, n_main = p.shape
    tm = min(512, l)
    halo = 16
    nh = l // halo
    r = tm // halo
    return pl.pallas_call(
        functools.partial(_conv_prep_kernel, tm=tm, halo=halo),
        grid=(b, l // tm),
        in_specs=[pl.BlockSpec((1, tm, n_main), lambda i, j: (i, j, 0)),
                  pl.BlockSpec((1, halo, n_main), lambda i, j: (i, jnp.maximum(j * r - 1, 0), 0)),
                  pl.BlockSpec((1, halo, n_main), lambda i, j: (i, jnp.minimum((j + 1) * r, nh - 1), 0)),
                  pl.BlockSpec((1, tm, LANES), lambda i, j: (i, j, 0)),
                  pl.BlockSpec(conv_a.shape, lambda i, j: (0, 0)),
                  pl.BlockSpec(conv_qkv.shape, lambda i, j: (0, 0)),
                  pl.BlockSpec((1, LANES), lambda i, j: (0, 0)),
                  pl.BlockSpec((1, LANES), lambda i, j: (0, 0))],
        out_specs=[pl.BlockSpec((1, tm, CONV_DIM), lambda i, j: (i, j, 0)),
                   pl.BlockSpec((1, tm, 3 * CONV_DIM), lambda i, j: (i, j, 0)),
                   pl.BlockSpec((1, tm, LANES), lambda i, j: (i, j, 0))],
        out_shape=[jax.ShapeDtypeStruct((b, l, CONV_DIM), BF16),
                   jax.ShapeDtypeStruct((b, l, 3 * CONV_DIM), BF16),
                   jax.ShapeDtypeStruct((b, l, LANES), F32)],
        compiler_params=_cparams(("arbitrary", "arbitrary")),
        name="l0_conv_prep",
    )(p, p, p, gates, conv_a, conv_qkv, alog_vec, dt_vec)


def _gdn_kernel(qkv_ref, gp_ref, s0_ref, o_ref, sout_ref, state, *, rev):
    i = pl.program_id(1)
    n = GDN_GROUP
    c = GDN_CHUNK
    nc = n // c
    dirn = 1 if rev else 0

    @pl.when(i == 0)
    def _():
        state[...] = s0_ref[0]

    ri = lax.broadcasted_iota(I32, (n, n), 0)
    ci = lax.broadcasted_iota(I32, (n, n), 1)
    same = (ri // c) == (ci // c)
    if rev:
        incl = jnp.logical_and(same, ci >= ri)
        strict = jnp.logical_and(same, ci > ri)
    else:
        incl = jnp.logical_and(same, ci <= ri)
        strict = jnp.logical_and(same, ci < ri)
    eye = jnp.where(ri == ci, 1.0, 0.0).astype(F32)

    gp = gp_ref[0]
    gc_all = _dot_exact_lhs(jnp.where(incl, 1.0, 0.0).astype(BF16), gp)
    gt_all = _dot_exact_lhs(jnp.where(same, 1.0, 0.0).astype(BF16), gp)
    gc_rows = gc_all.T

    for hd in range(GDN_HEADS):
        gcol = dirn * GDN_HEADS + hd
        bcol = 2 * GDN_HEADS + gcol
        gc = gc_all[:, gcol:gcol + 1]
        gcr = gc_rows[gcol:gcol + 1, :]
        gt = gt_all[:, gcol:gcol + 1]
        beta = gp[:, bcol:bcol + 1]
        q16 = qkv_ref[0, :, hd * GDN_DK:(hd + 1) * GDN_DK]
        k16 = qkv_ref[0, :, (GDN_HEADS + hd) * GDN_DK:(GDN_HEADS + hd + 1) * GDN_DK]
        v16 = qkv_ref[0, :, (2 * GDN_HEADS + hd) * GDN_DK:(2 * GDN_HEADS + hd + 1) * GDN_DK]
        q = q16.astype(F32)
        k = k16.astype(F32)
        v = v16.astype(F32)

        decay = jnp.where(incl, jnp.exp(jnp.where(incl, gc - gcr, 0.0)), 0.0)
        kb = k * beta
        a = jnp.where(strict, _dot_nt(kb.astype(BF16), k16) * decay, 0.0)
        t = eye - a
        a16 = a.astype(BF16)
        pw = _dot(a16, a16)
        for step in range(5):
            pw16 = pw.astype(BF16)
            t = t + _dot(t.astype(BF16), pw16)
            if step < 4:
                pw = _dot(pw16, pw16)

        ex = jnp.exp(gc)
        rhs = jnp.concatenate([(v * beta).astype(BF16), (kb * ex).astype(BF16)], axis=1)
        uw = _dot(t.astype(BF16), rhs)
        u = uw[:, :GDN_DK]
        w = uw[:, GDN_DK:].astype(BF16)
        intra = jnp.where(incl, _dot_nt(q16, k16) * decay, 0.0).astype(BF16)
        qd = (q * ex).astype(BF16)
        kd = (k * jnp.exp(gt - gc)).astype(BF16)

        s = state[hd]
        order = range(nc - 1, -1, -1) if rev else range(nc)
        for cc in order:
            r0 = cc * c
            sb = s.astype(BF16)
            vn = u[r0:r0 + c] - _dot(w[r0:r0 + c], sb)
            vn16 = vn.astype(BF16)
            o_c = _dot(qd[r0:r0 + c], sb) + _dot(intra[r0:r0 + c, r0:r0 + c], vn16)
            gl = jnp.exp(gt[r0:r0 + 1, :])
            s = s * gl + _dot_tn(kd[r0:r0 + c], vn16)
            o_ref[0, r0:r0 + c, hd * GDN_DK:(hd + 1) * GDN_DK] = o_c
        state[hd] = s

    @pl.when(i == pl.num_programs(1) - 1)
    def _():
        sout_ref[0] = state[...]


def _gdn_scan(qkv, gp, s0, rev):
    b, l, _ = qkv.shape
    n = GDN_GROUP
    g = l // n
    hdv = GDN_HEADS * GDN_DK
    if rev:
        idx = lambda i, j: (i, g - 1 - j, 0)
    else:
        idx = lambda i, j: (i, j, 0)
    return pl.pallas_call(
        functools.partial(_gdn_kernel, rev=rev),
        grid=(b, g),
        in_specs=[pl.BlockSpec((1, n, 3 * hdv), idx),
                  pl.BlockSpec((1, n, LANES), idx),
                  pl.BlockSpec((1, GDN_HEADS, GDN_DK, GDN_DK), lambda i, j: (i, 0, 0, 0))],
        out_specs=[pl.BlockSpec((1, n, hdv), idx),
                   pl.BlockSpec((1, GDN_HEADS, GDN_DK, GDN_DK), lambda i, j: (i, 0, 0, 0))],
        out_shape=[jax.ShapeDtypeStruct((b, l, hdv), F32),
                   jax.ShapeDtypeStruct((b, GDN_HEADS, GDN_DK, GDN_DK), F32)],
        scratch_shapes=[pltpu.VMEM((GDN_HEADS, GDN_DK, GDN_DK), F32)],
        compiler_params=_cparams(("arbitrary", "arbitrary")),
        name="l0_gdn_bwd" if rev else "l0_gdn_fwd",
    )(qkv, gp, s0)


def _gdn_out_kernel(of_ref, ob_ref, z_ref, yc_ref, x_ref, gate_ref, gn_ref, w_ref, o_ref):
    o = of_ref[0] + ob_ref[0]
    z = z_ref[0].astype(F32)
    parts = []
    for hd in range(GDN_HEADS):
        oh = o[:, hd * GDN_DK:(hd + 1) * GDN_DK]
        ms = jnp.mean(oh * oh, axis=-1, keepdims=True)
        parts.append(oh * lax.rsqrt(ms + EPS) * gn_ref[...] * _silu(z[:, hd * GDN_DK:(hd + 1) * GDN_DK]))
    yg = jnp.concatenate(parts, axis=1).astype(BF16)
    cd = CONV_DIM
    y = _dot(yc_ref[0], w_ref[0:cd, :]) + _dot(yg, w_ref[cd:, :])
    o_ref[0] = x_ref[0] + gate_ref[0] * y


def _gdn_out(o_f, o_b, p, yconv, x, gate, gn, w_out):
    b, l, d = x.shape
    tm = min(512, l)
    hdv = GDN_HEADS * GDN_DK
    zblk = (3 * CONV_DIM + 3 * hdv) // hdv
    row = lambda i, j: (i, j, 0)
    return pl.pallas_call(
        _gdn_out_kernel,
        grid=(b, l // tm),
        in_specs=[pl.BlockSpec((1, tm, hdv), row),
                  pl.BlockSpec((1, tm, hdv), row),
                  pl.BlockSpec((1, tm, hdv), lambda i, j: (i, j, zblk)),
                  pl.BlockSpec((1, tm, CONV_DIM), row),
                  pl.BlockSpec((1, tm, d), row),
                  pl.BlockSpec((1, 1, d), lambda i, j: (i, 0, 0)),
                  pl.BlockSpec((1, GDN_DK), lambda i, j: (0, 0)),
                  pl.BlockSpec(w_out.shape, lambda i, j: (0, 0))],
        out_specs=pl.BlockSpec((1, tm, d), row),
        out_shape=jax.ShapeDtypeStruct((b, l, d), F32),
        compiler_params=_cparams(("arbitrary", "arbitrary")),
        name="l0_out_proj",
    )(o_f, o_b, p, yconv, x, gate, gn, w_out)


def _router_kernel(x_ref, sh_ref, sc_ref, nw_ref, rt_ref, hm3_ref, aff_ref, *, tm):
    hm = _modulate_tile(x_ref[0], nw_ref[...], sh_ref[0], sc_ref[0])
    for cc in range(SUBLANES):
        hm3_ref[0, pl.ds(cc, tm, stride=SUBLANES), :] = hm[:, cc * LANES:(cc + 1) * LANES]
    rh, rl = _split2(rt_ref[...])
    hh, hl = _split2(hm)
    logits = _dot_nt(rh, hh) + _dot_nt(rh, hl) + _dot_nt(rl, hh)
    m = jnp.max(logits, axis=0, keepdims=True)
    e = jnp.exp(logits - m)
    aff_ref[0] = e / jnp.sum(e, axis=0, keepdims=True)


def _router(x, shift, scale, nw, router_t):
    b, l, d = x.shape
    tm = min(512, l)
    return pl.pallas_call(
        functools.partial(_router_kernel, tm=tm),
        grid=(b, l // tm),
        in_specs=[pl.BlockSpec((1, tm, d), lambda i, j: (i, j, 0)),
                  pl.BlockSpec((1, 1, d), lambda i, j: (i, 0, 0)),
                  pl.BlockSpec((1, 1, d), lambda i, j: (i, 0, 0)),
                  pl.BlockSpec((1, d), lambda i, j: (0, 0)),
                  pl.BlockSpec((N_EXPERTS, d), lambda i, j: (0, 0))],
        out_specs=[pl.BlockSpec((1, tm * SUBLANES, LANES), lambda i, j: (i, j, 0)),
                   pl.BlockSpec((1, N_EXPERTS, tm), lambda i, j: (i, 0, j))],
        out_shape=[jax.ShapeDtypeStruct((b, l * SUBLANES, LANES), F32),
                   jax.ShapeDtypeStruct((b, N_EXPERTS, l), F32)],
        compiler_params=_cparams(("arbitrary", "arbitrary")),
        name="moe_router",
    )(x, shift, scale, nw, router_t)


def _topk_kernel(aff_ref, idx_ref, gate_ref, *, cap, capp):
    r = TOPK_ROWS
    a3 = aff_ref[0]
    bits3 = pltpu.bitcast(a3, I32)

    def count_ge(cand):
        hit = jnp.where(bits3 >= cand, 1.0, 0.0)
        return jnp.sum(jnp.sum(hit, axis=1, keepdims=True), axis=2, keepdims=True)

    thr = jnp.zeros((N_EXPERTS, 1, 1), I32)
    for bit in range(30, -1, -1):
        cand = thr | (1 << bit)
        thr = jnp.where(count_ge(cand) >= cap, cand, thr)

    li = lax.broadcasted_iota(I32, (LANES, LANES), 0)
    lj = lax.broadcasted_iota(I32, (LANES, LANES), 1)
    tri_incl = jnp.where(li <= lj, 1.0, 0.0).astype(BF16)
    tri_strict_rows = jnp.where(lj < li, 1.0, 0.0).astype(BF16)
    ones8 = jnp.ones((SUBLANES, LANES), BF16)
    lane_row = lax.broadcasted_iota(I32, (capp, LANES), 1).astype(F32)
    slot = lax.broadcasted_iota(I32, (capp, 1), 0).astype(F32)

    for ex in range(N_EXPERTS):
        a = aff_ref[0, ex]
        bits = pltpu.bitcast(a, I32)
        th = thr[ex]
        gt = bits > th
        eq = bits == th
        n_gt = jnp.sum(jnp.sum(jnp.where(gt, 1.0, 0.0), axis=0, keepdims=True), axis=1, keepdims=True)
        need = cap - n_gt
        eq16 = jnp.where(eq, 1.0, 0.0).astype(BF16)
        eq_lc = _dot(eq16, tri_incl)
        eq_cnt = jnp.broadcast_to(eq_lc[:, LANES - 1:LANES], (r, LANES)).astype(BF16)
        eq_off = _dot(tri_strict_rows, eq_cnt)
        take_eq = jnp.logical_and(eq, (eq_off + eq_lc) <= need)
        sel16 = jnp.where(jnp.logical_or(gt, take_eq), 1.0, 0.0).astype(BF16)

        lc = _dot(sel16, tri_incl)
        cnt_row = _dot_nt(ones8, sel16)
        off_incl = _dot(cnt_row.astype(BF16), tri_incl)
        off_excl = off_incl - cnt_row
        jrow = jnp.sum(jnp.where(off_incl[0:1, :] <= slot, 1.0, 0.0), axis=1, keepdims=True)
        onehot = lane_row == jrow
        onehot16 = jnp.where(onehot, 1.0, 0.0).astype(BF16)
        lc_rows = _dot(onehot16, lc.astype(BF16))
        off_s = jnp.sum(jnp.where(onehot, off_excl[0:1, :], 0.0), axis=1, keepdims=True)
        rank = slot - off_s
        pos = jnp.sum(jnp.where(lc_rows <= rank, 1.0, 0.0), axis=1, keepdims=True)
        tok = jrow * float(LANES) + pos
        ah, am, al = _split3(a)
        a_rows = _dot(onehot16, ah) + _dot(onehot16, am) + _dot(onehot16, al)
        gate = jnp.sum(jnp.where(lane_row == pos, a_rows, 0.0), axis=1, keepdims=True)
        tok_t = jnp.broadcast_to(tok, (capp, LANES)).T
        gate_t = jnp.broadcast_to(gate, (capp, LANES)).T
        idx_ref[0, ex] = tok_t[0:1, :].astype(I32)
        gate_ref[0, ex] = gate_t[0:1, :]


def _topk(aff_t, cap):
    b, e, n = aff_t.shape
    npad = TOPK_ROWS * LANES
    capp = max(cap, LANES)
    a = jnp.pad(aff_t, ((0, 0), (0, 0), (0, npad - n)), constant_values=-1.0).reshape(b, e, TOPK_ROWS, LANES)
    idx, gate = pl.pallas_call(
        functools.partial(_topk_kernel, cap=cap, capp=capp),
        grid=(b,),
        in_specs=[pl.BlockSpec((1, e, TOPK_ROWS, LANES), lambda i: (i, 0, 0, 0))],
        out_specs=[pl.BlockSpec((1, e, 1, capp), lambda i: (i, 0, 0, 0)),
                   pl.BlockSpec((1, e, 1, capp), lambda i: (i, 0, 0, 0))],
        out_shape=[jax.ShapeDtypeStruct((b, e, 1, capp), I32),
                   jax.ShapeDtypeStruct((b, e, 1, capp), F32)],
        compiler_params=_cparams(("arbitrary",)),
        name="moe_topk",
    )(a)
    return idx[:, :, 0, :cap], gate[:, :, 0, :cap]


def _gather_kernel(idx_ref, hm3_hbm, xe_ref, stage, sem, *, cap):
    b = pl.program_id(0)

    def row_copy(s):
        t = idx_ref[0, 0, s]
        src = hm3_hbm.at[b, pl.ds(pl.multiple_of(t * SUBLANES, SUBLANES), SUBLANES), :]
        dst = stage.at[pl.ds(pl.multiple_of(s * SUBLANES, SUBLANES), SUBLANES), :]
        return pltpu.make_async_copy(src, dst, sem)

    def issue(g, carry):
        for u in range(SUBLANES):
            row_copy(g * SUBLANES + u).start()
        return carry

    lax.fori_loop(0, cap // SUBLANES, issue, 0)
    pltpu.make_async_copy(hm3_hbm.at[b, pl.ds(0, cap * SUBLANES), :], stage, sem).wait()
    for cc in range(SUBLANES):
        xe_ref[0, 0, :, cc * LANES:(cc + 1) * LANES] = stage[pl.ds(cc, cap, stride=SUBLANES), :].astype(BF16)


def _gather(idx, hm3, d):
    b, e, cap = idx.shape
    return pl.pallas_call(
        functools.partial(_gather_kernel, cap=cap),
        grid=(b, e),
        in_specs=[pl.BlockSpec((1, 1, cap), lambda i, j: (i * e + j, 0, 0), memory_space=pltpu.SMEM),
                  pl.BlockSpec(memory_space=pl.ANY)],
        out_specs=pl.BlockSpec((1, 1, cap, d), lambda i, j: (i, j, 0, 0)),
        out_shape=jax.ShapeDtypeStruct((b, e, cap, d), BF16),
        scratch_shapes=[pltpu.VMEM((cap * SUBLANES, LANES), F32), pltpu.SemaphoreType.DMA(())],
        compiler_params=_cparams(("arbitrary", "arbitrary")),
        name="moe_gather",
    )(idx.reshape(b * e, 1, cap), hm3)


def _ffn_kernel(x_ref, wg_ref, wu_ref, wd_ref, y3_ref, acc, *, cap):
    f = pl.program_id(2)
    x = x_ref[0, 0]
    g = _dot(x, wg_ref[0].astype(BF16))
    u = _dot(x, wu_ref[0].astype(BF16))
    h = (_silu(g) * u).astype(BF16)
    part = _dot(h, wd_ref[0].astype(BF16))

    @pl.when(f == 0)
    def _():
        acc[...] = part

    @pl.when(f > 0)
    def _():
        acc[...] += part

    @pl.when(f == pl.num_programs(2) - 1)
    def _():
        for cc in range(SUBLANES):
            y3_ref[0, 0, pl.ds(cc, cap, stride=SUBLANES), :] = acc[:, cc * LANES:(cc + 1) * LANES]


def _ffn(xe, wg, wu, wd):
    b, e, cap, d = xe.shape
    ff = wg.shape[2]
    tf = 512
    return pl.pallas_call(
        functools.partial(_ffn_kernel, cap=cap),
        grid=(b, e, ff // tf),
        in_specs=[pl.BlockSpec((1, 1, cap, d), lambda i, j, k: (i, j, 0, 0)),
                  pl.BlockSpec((1, d, tf), lambda i, j, k: (j, 0, k)),
                  pl.BlockSpec((1, d, tf), lambda i, j, k: (j, 0, k)),
                  pl.BlockSpec((1, tf, d), lambda i, j, k: (j, k, 0))],
        out_specs=pl.BlockSpec((1, 1, cap * SUBLANES, LANES), lambda i, j, k: (i, j, 0, 0)),
        out_shape=jax.ShapeDtypeStruct((b, e, cap * SUBLANES, LANES), F32),
        scratch_shapes=[pltpu.VMEM((cap, d), F32)],
        compiler_params=_cparams(("arbitrary", "arbitrary", "arbitrary")),
        name="moe_ffn",
    )(xe, wg, wu, wd)


def _combine_kernel(idx_ref, gate_ref, y3_ref, out_hbm, acc, sem, *, cap, unroll):
    b = pl.program_id(0)
    e = pl.program_id(1)

    @pl.when(e == 0)
    def _():
        acc[...] = jnp.zeros_like(acc)

    def body(g, carry):
        base = g * unroll
        rows = []
        for u in range(unroll):
            s = base + u
            t = pl.multiple_of(idx_ref[0, 0, s] * SUBLANES, SUBLANES)
            y = y3_ref[0, 0, pl.ds(pl.multiple_of(s * SUBLANES, SUBLANES), SUBLANES), :]
            rows.append((t, acc[pl.ds(t, SUBLANES), :] + gate_ref[0, 0, s] * y))
        for t, val in rows:
            acc[pl.ds(t, SUBLANES), :] = val
        return carry

    lax.fori_loop(0, cap // unroll, body, 0)

    @pl.when(e == pl.num_programs(1) - 1)
    def _():
        cp = pltpu.make_async_copy(acc, out_hbm.at[b], sem)
        cp.start()
        cp.wait()


def _combine(idx, gate, y3, l):
    b, e, cap = idx.shape
    return pl.pallas_call(
        functools.partial(_combine_kernel, cap=cap, unroll=4),
        grid=(b, e),
        in_specs=[pl.BlockSpec((1, 1, cap), lambda i, j: (i * e + j, 0, 0), memory_space=pltpu.SMEM),
                  pl.BlockSpec((1, 1, cap), lambda i, j: (i * e + j, 0, 0), memory_space=pltpu.SMEM),
                  pl.BlockSpec((1, 1, cap * SUBLANES, LANES), lambda i, j: (i, j, 0, 0))],
        out_specs=pl.BlockSpec(memory_space=pl.ANY),
        out_shape=jax.ShapeDtypeStruct((b, l * SUBLANES, LANES), F32),
        scratch_shapes=[pltpu.VMEM((l * SUBLANES, LANES), F32), pltpu.SemaphoreType.DMA(())],
        compiler_params=_cparams(("arbitrary", "arbitrary")),
        name="moe_combine",
    )(idx.reshape(b * e, 1, cap), gate.reshape(b * e, 1, cap), y3)


def _moe_resid_kernel(x_ref, acc3_ref, gate_ref, fw_ref, o_ref, *, tm, final):
    parts = [acc3_ref[0, pl.ds(cc, tm, stride=SUBLANES), :] for cc in range(SUBLANES)]
    y = x_ref[0] + gate_ref[0] * jnp.concatenate(parts, axis=1)
    if final:
        ms = jnp.mean(y * y, axis=-1, keepdims=True)
        y = y * lax.rsqrt(ms + EPS) * fw_ref[...]
    o_ref[0] = y


def _moe_resid(x, acc3, gate, final_w, final):
    b, l, d = x.shape
    tm = min(512, l)
    return pl.pallas_call(
        functools.partial(_moe_resid_kernel, tm=tm, final=final),
        grid=(b, l // tm),
        in_specs=[pl.BlockSpec((1, tm, d), lambda i, j: (i, j, 0)),
                  pl.BlockSpec((1, tm * SUBLANES, LANES), lambda i, j: (i, j, 0)),
                  pl.BlockSpec((1, 1, d), lambda i, j: (i, 0, 0)),
                  pl.BlockSpec((1, d), lambda i, j: (0, 0))],
        out_specs=pl.BlockSpec((1, tm, d), lambda i, j: (i, j, 0)),
        out_shape=jax.ShapeDtypeStruct((b, l, d), F32),
        compiler_params=_cparams(("arbitrary", "arbitrary")),
        name="moe_resid",
    )(x, acc3, gate, final_w)


def _moe(x, shift, scale, gate, nw, router_t, wg, wu, wd, final_w, final):
    b, l, d = x.shape
    cap = CAPACITY_FACTOR * l // N_EXPERTS
    hm3, aff_t = _router(x, shift, scale, nw, router_t)
    idx, gates = _topk(aff_t, cap)
    xe = _gather(idx, hm3, d)
    y3 = _ffn(xe, wg, wu, wd)
    acc3 = _combine(idx, gates, y3, l)
    return _moe_resid(x, acc3, gate, final_w, final)


def _qkv_kernel(x_ref, sh_ref, sc_ref, nw_ref, w_ref, cos_ref, sin_ref, q_ref, k_ref, vt_ref, *, rope, qscale):
    h = _modulate_tile(x_ref[0], nw_ref[...], sh_ref[0], sc_ref[0]).astype(BF16)
    dq = q_ref.shape[2]
    chunk = 512
    rep = chunk // LANES
    if rope:
        cos = jnp.concatenate([cos_ref[...]] * rep, axis=1)
        sin = jnp.concatenate([sin_ref[...]] * rep, axis=1)
        lane = lax.broadcasted_iota(I32, (x_ref.shape[1], chunk), 1)
        first = (lane % 32) < 16
    hd = 2 * DIFF_DH
    for which, ref in enumerate((q_ref, k_ref, vt_ref)):
        for c0 in range(0, dq, chunk):
            y = _dot(h, w_ref[:, which * dq + c0:which * dq + c0 + chunk])
            if rope and which < 2:
                partner = jnp.where(first, pltpu.roll(y, chunk - 16, axis=1), pltpu.roll(y, 16, axis=1))
                y = y * cos + partner * sin
            if which == 0:
                y = y * qscale
            if which < 2:
                ref[0, :, c0:c0 + chunk] = y.astype(BF16)
            else:
                for c1 in range(0, chunk, hd):
                    ref[0, c0 + c1:c0 + c1 + hd, :] = y[:, c1:c1 + hd].T.astype(BF16)


def _qkv_proj(x, shift, scale, nw, w, cos_t, sin_t, rope):
    b, l, d = x.shape
    tm = min(512, l)
    dq = w.shape[1] // 3
    qscale = (DIFF_DH ** -0.5) * LOG2E
    row = lambda i, j: (i, j, 0)
    out = jax.ShapeDtypeStruct((b, l, dq), BF16)
    return pl.pallas_call(
        functools.partial(_qkv_kernel, rope=rope, qscale=qscale),
        grid=(b, l // tm),
        in_specs=[pl.BlockSpec((1, tm, d), row),
                  pl.BlockSpec((1, 1, d), lambda i, j: (i, 0, 0)),
                  pl.BlockSpec((1, 1, d), lambda i, j: (i, 0, 0)),
                  pl.BlockSpec((1, d), lambda i, j: (0, 0)),
                  pl.BlockSpec(w.shape, lambda i, j: (0, 0)),
                  pl.BlockSpec((tm, LANES), lambda i, j: (j, 0)),
                  pl.BlockSpec((tm, LANES), lambda i, j: (j, 0))],
        out_specs=[pl.BlockSpec((1, tm, dq), row), pl.BlockSpec((1, tm, dq), row),
                   pl.BlockSpec((1, dq, tm), lambda i, j: (i, 0, j))],
        out_shape=[out, out, jax.ShapeDtypeStruct((b, dq, l), BF16)],
        compiler_params=_cparams(("arbitrary", "arbitrary")),
        name="l1_qkv_rope" if rope else "l1_qkv_ctx",
    )(x, shift, scale, nw, w, cos_t, sin_t)


def _rope_tables(l):
    t = np.arange(l)
    n_freq = DIFF_DH // 4
    inv_freq = ROPE_BASE ** (-np.arange(n_freq, dtype=np.float32) / n_freq)
    ang_row = (t // GRID_W).astype(np.float32)[:, None] * inv_freq
    ang_col = (t % GRID_W).astype(np.float32)[:, None] * inv_freq
    ang = np.concatenate([ang_row, ang_row, ang_col, ang_col], axis=1)
    sign = np.concatenate([-np.ones(n_freq), np.ones(n_freq)] * 2).astype(np.float32)
    cos = np.cos(ang).astype(np.float32)
    sin = (np.sin(ang) * sign).astype(np.float32)
    return jnp.asarray(np.concatenate([cos, cos], axis=1)), jnp.asarray(np.concatenate([sin, sin], axis=1))


def _attn_kernel(q_ref, k_ref, vt_ref, lam_ref, sub_ref, o_ref, *, tq, tk, lam_init):
    lq = q_ref.shape[1]
    lk = k_ref.shape[1]
    dv = vt_ref.shape[1]
    nk = lk // tk
    lp = lam_ref[...]
    lam = (jnp.exp(jnp.sum(lp[0:1] * lp[1:2], axis=1, keepdims=True))
           - jnp.exp(jnp.sum(lp[2:3] * lp[3:4], axis=1, keepdims=True)) + lam_init)

    def update(s, vt, m, l, acc):
        mn = jnp.maximum(m, jnp.max(s, axis=0, keepdims=True))
        alpha = jnp.exp2(m - mn)
        p = jnp.exp2(s - mn)
        l = alpha * l + jnp.sum(p, axis=0, keepdims=True)
        acc = alpha * acc + _dot(vt, p.astype(BF16))
        return mn, l, acc

    def q_tile(i, carry):
        r0 = pl.multiple_of(i * tq, tq)
        q = q_ref[0, pl.ds(r0, tq), :]
        lane = lax.broadcasted_iota(I32, q.shape, 1)
        zero = jnp.zeros_like(q)
        q1 = jnp.where(lane < DIFF_DH, q, zero)
        q2 = jnp.where(lane >= DIFF_DH, q, zero)
        m1 = m2 = jnp.full((1, tq), -1e30, F32)
        l1 = l2 = jnp.zeros((1, tq), F32)
        a1 = a2 = jnp.zeros((dv, tq), F32)
        for j in range(nk):
            k = k_ref[0, j * tk:(j + 1) * tk, :]
            vt = vt_ref[0, :, j * tk:(j + 1) * tk]
            m1, l1, a1 = update(_dot_nt(k, q1), vt, m1, l1, a1)
            m2, l2, a2 = update(_dot_nt(k, q2), vt, m2, l2, a2)
        o = a1 / l1 - lam * (a2 / l2)
        ms = jnp.mean(o * o, axis=0, keepdims=True)
        o = (o * lax.rsqrt(ms + EPS)).T * (sub_ref[...] * (1.0 - lam_init))
        o_ref[0, pl.ds(r0, tq), :] = o.astype(BF16)
        return carry

    lax.fori_loop(0, lq // tq, q_tile, 0)


def _attention(q, k_all, vt_all, lam_params, subln, lam_init):
    b, lq, dq = q.shape
    lk = k_all.shape[1]
    hd = 2 * DIFF_DH
    tq = min(256, lq)
    tk = 768 if lk % 768 == 0 else 256
    return pl.pallas_call(
        functools.partial(_attn_kernel, tq=tq, tk=tk, lam_init=lam_init),
        grid=(b, DIFF_HEADS),
        in_specs=[pl.BlockSpec((1, lq, hd), lambda i, h: (i, 0, h)),
                  pl.BlockSpec((1, lk, hd), lambda i, h: (i, 0, h)),
                  pl.BlockSpec((1, hd, lk), lambda i, h: (i, h, 0)),
                  pl.BlockSpec(lam_params.shape, lambda i, h: (0, 0)),
                  pl.BlockSpec((1, hd), lambda i, h: (0, 0))],
        out_specs=pl.BlockSpec((1, lq, hd), lambda i, h: (i, 0, h)),
        out_shape=jax.ShapeDtypeStruct((b, lq, dq), BF16),
        compiler_params=_cparams(("arbitrary", "arbitrary")),
        name="l1_diff_attn",
    )(q, k_all, vt_all, lam_params, subln)


def _resid_matmul_kernel(a_ref, w_ref, x_ref, gate_ref, o_ref):
    o_ref[0] = x_ref[0] + gate_ref[0] * _dot(a_ref[0], w_ref[...])


def _resid_matmul(a, w, x, gate):
    b, l, d = x.shape
    tm = min(512, l)
    row = lambda i, j: (i, j, 0)
    return pl.pallas_call(
        _resid_matmul_kernel,
        grid=(b, l // tm),
        in_specs=[pl.BlockSpec((1, tm, a.shape[2]), row),
                  pl.BlockSpec(w.shape, lambda i, j: (0, 0)),
                  pl.BlockSpec((1, tm, d), row),
                  pl.BlockSpec((1, 1, d), lambda i, j: (i, 0, 0))],
        out_specs=pl.BlockSpec((1, tm, d), row),
        out_shape=jax.ShapeDtypeStruct((b, l, d), F32),
        compiler_params=_cparams(("arbitrary", "arbitrary")),
        name="l1_out_proj",
    )(a, w, x, gate)


def _mod_vectors(c, c_ctx, w, b):
    bn, d = c.shape
    cond = jnp.concatenate([c, c_ctx[None], jnp.zeros((8 - bn - 1, d), F32)], axis=0)
    m = _adaln(cond, w, b)
    lat = [m[:bn, i * d:(i + 1) * d][:, None, :] for i in range(6)]
    ctx = [jnp.broadcast_to(m[bn:bn + 1, i * d:(i + 1) * d][:, None, :], (bn, 1, d)) for i in range(6)]
    return lat, ctx


def _row(v):
    return v.reshape(1, -1)


def kernel(x, c, ctx, c_ctx, l0_mod_w, l0_mod_b, l0_norm1, l0_w_in, l0_conv_a, l0_conv_qkv, l0_a_log, l0_dt_bias, l0_gdn_norm, l0_w_out, l0_norm2, l0_router, l0_exp_gate, l0_exp_up, l0_exp_down, l1_mod_w, l1_mod_b, l1_norm1, l1_w_qkv, l1_lambda_q1, l1_lambda_k1, l1_lambda_q2, l1_lambda_k2, l1_subln, l1_w_o, l1_norm2, l1_router, l1_exp_gate, l1_exp_up, l1_exp_down, final_norm):
    bn, seq, d = x.shape
    n_main = 3 * CONV_DIM + 4 * GDN_HEADS * GDN_DK
    n_gate = 4 * GDN_HEADS

    m, mc = _mod_vectors(c, c_ctx, l0_mod_w, l0_mod_b)
    w_in = jnp.concatenate([l0_w_in, jnp.zeros((d, LANES - n_gate), F32)], axis=1).astype(BF16)
    w_out = l0_w_out.astype(BF16)
    pad = jnp.zeros((LANES - 2 * GDN_HEADS,), F32)
    alog_vec = jnp.concatenate([l0_a_log.reshape(-1), pad]).reshape(1, LANES)
    dt_vec = jnp.concatenate([l0_dt_bias.reshape(-1), pad]).reshape(1, LANES)
    router0_t = l0_router.T

    def mixer0(tok, mod, s0_f, s0_b):
        p, gates = _in_proj(tok, mod[0], mod[1], _row(l0_norm1), w_in, n_main)
        yconv, qkv, gp = _conv_prep(p, gates, l0_conv_a, l0_conv_qkv, alog_vec, dt_vec)
        o_f, s_f = _gdn_scan(qkv, gp, s0_f, rev=False)
        o_b, s_b = _gdn_scan(qkv, gp, s0_b, rev=True)
        out = _gdn_out(o_f, o_b, p, yconv, tok, mod[2], _row(l0_gdn_norm), w_out)
        return out, s_f, s_b

    s0 = jnp.zeros((bn, GDN_HEADS, GDN_DK, GDN_DK), F32)
    ctx1, s_f, s_b = mixer0(ctx, mc, s0, s0)
    x1, _, _ = mixer0(x, m, s_f, s_b)
    moe0 = functools.partial(_moe, nw=_row(l0_norm2), router_t=router0_t, wg=l0_exp_gate, wu=l0_exp_up,
                             wd=l0_exp_down, final_w=_row(final_norm), final=False)
    x2 = moe0(x1, m[3], m[4], m[5])
    ctx2 = moe0(ctx1, mc[3], mc[4], mc[5])

    m, mc = _mod_vectors(c, c_ctx, l1_mod_w, l1_mod_b)
    lam_init = 0.8 - 0.6 * math.exp(-0.3 * 1)
    w_qkv = l1_w_qkv.astype(BF16)
    cos_t, sin_t = _rope_tables(seq)
    q, k_lat, vt_lat = _qkv_proj(x2, m[0], m[1], _row(l1_norm1), w_qkv, cos_t, sin_t, rope=True)
    lc = ctx2.shape[1]
    _, k_ctx, vt_ctx = _qkv_proj(ctx2, mc[0], mc[1], _row(l1_norm1), w_qkv, cos_t[:lc], sin_t[:lc], rope=False)
    k_all = jnp.concatenate([k_ctx, k_lat], axis=1)
    vt_all = jnp.concatenate([vt_ctx, vt_lat], axis=2)
    lam_params = jnp.stack([l1_lambda_q1, l1_lambda_k1, l1_lambda_q2, l1_lambda_k2])
    o = _attention(q, k_all, vt_all, lam_params, _row(l1_subln), lam_init)
    x3 = _resid_matmul(o, l1_w_o.astype(BF16), x2, m[2])
    return _moe(x3, m[3], m[4], m[5], _row(l1_norm2), l1_router.T, l1_exp_gate, l1_exp_up, l1_exp_down,
                _row(final_norm), True)
```

```python
import functools
import math

import jax
import jax.numpy as jnp
import numpy as np
from jax import lax
from jax.experimental import pallas as pl
from jax.experimental.pallas import tpu as pltpu

F32 = jnp.float32
BF16 = jnp.bfloat16
I32 = jnp.int32

EPS = 1e-6
GRID_W = 64
ROPE_BASE = 10000.0
CONV_DIM = 512
GDN_HEADS = 4
GDN_DK = 128
GDN_CHUNK = 64
GDN_GROUP = 256
DIFF_HEADS = 8
DIFF_DH = 64
N_EXPERTS = 16
CAPACITY_FACTOR = 2
LANES = 128
SUBLANES = 8
TOPK_ROWS = 128
V7X_VMEM_LIMIT = 56 * 1024 * 1024
LOG2E = 1.4426950408889634


def _cparams(sem, vmem=None):
    return pltpu.CompilerParams(dimension_semantics=sem, vmem_limit_bytes=vmem or V7X_VMEM_LIMIT)


def _dot(a, b):
    return jnp.dot(a, b, preferred_element_type=F32)


def _dot_nt(a, b):
    return lax.dot_general(a, b, (((1,), (1,)), ((), ())), preferred_element_type=F32)


def _dot_tn(a, b):
    return lax.dot_general(a, b, (((0,), (0,)), ((), ())), preferred_element_type=F32)


def _split2(x):
    hi = x.astype(BF16)
    lo = (x - hi.astype(F32)).astype(BF16)
    return hi, lo


def _split3(x):
    hi = x.astype(BF16)
    r = x - hi.astype(F32)
    mid = r.astype(BF16)
    lo = (r - mid.astype(F32)).astype(BF16)
    return hi, mid, lo


def _dot_exact_lhs(m_bf16, x):
    hi, mid, lo = _split3(x)
    return _dot(m_bf16, hi) + _dot(m_bf16, mid) + _dot(m_bf16, lo)


def _mm3(a, b):
    ah, al = _split2(a)
    bh, bl = _split2(b)
    return _dot(ah, bh) + _dot(ah, bl) + _dot(al, bh)


def _silu(x):
    return x * jax.nn.sigmoid(x)


def _modulate_tile(x, nw, shift, scale):
    ms = jnp.mean(x * x, axis=-1, keepdims=True)
    return (x * lax.rsqrt(ms + EPS)) * nw * (1.0 + scale) + shift


def _adaln_kernel(c_ref, w_ref, b_ref, o_ref):
    s = _silu(c_ref[...])
    o_ref[...] = _dot(s.astype(BF16), w_ref[...].astype(BF16)) + b_ref[...]


def _adaln(cond8, w, b):
    d, n = w.shape
    tn = 1536
    return pl.pallas_call(
        _adaln_kernel,
        grid=(n // tn,),
        in_specs=[pl.BlockSpec((8, d), lambda j: (0, 0)),
                  pl.BlockSpec((d, tn), lambda j: (0, j)),
                  pl.BlockSpec((1, tn), lambda j: (0, j))],
        out_specs=pl.BlockSpec((8, tn), lambda j: (0, j)),
        out_shape=jax.ShapeDtypeStruct((8, n), F32),
        compiler_params=_cparams(("arbitrary",)),
        name="adaln",
    )(cond8, w, b.reshape(1, n))


def _in_proj_kernel(x_ref, sh_ref, sc_ref, nw_ref, w_ref, p_ref, g_ref, *, n_main, chunk):
    h = _modulate_tile(x_ref[0], nw_ref[...], sh_ref[0], sc_ref[0]).astype(BF16)
    for n0 in range(0, n_main, chunk):
        p_ref[0, :, n0:n0 + chunk] = _dot(h, w_ref[:, n0:n0 + chunk]).astype(BF16)
    g_ref[0] = _dot(h, w_ref[:, n_main:])


def _in_proj(x, shift, scale, nw, w_pad, n_main):
    b, l, d = x.shape
    tm = min(512, l)
    n_all = w_pad.shape[1]
    return pl.pallas_call(
        functools.partial(_in_proj_kernel, n_main=n_main, chunk=512),
        grid=(b, l // tm),
        in_specs=[pl.BlockSpec((1, tm, d), lambda i, j: (i, j, 0)),
                  pl.BlockSpec((1, 1, d), lambda i, j: (i, 0, 0)),
                  pl.BlockSpec((1, 1, d), lambda i, j: (i, 0, 0)),
                  pl.BlockSpec((1, d), lambda i, j: (0, 0)),
                  pl.BlockSpec((d, n_all), lambda i, j: (0, 0))],
        out_specs=[pl.BlockSpec((1, tm, n_main), lambda i, j: (i, j, 0)),
                   pl.BlockSpec((1, tm, LANES), lambda i, j: (i, j, 0))],
        out_shape=[jax.ShapeDtypeStruct((b, l, n_main), BF16),
                   jax.ShapeDtypeStruct((b, l, LANES), F32)],
        compiler_params=_cparams(("arbitrary", "arbitrary")),
        name="l0_in_proj",
    )(x, shift, scale, nw, w_pad)


def _conv_prep_kernel(p_ref, pp_ref, pn_ref, g_ref, ca_ref, cq_ref, al_ref, dt_ref,
                      yc_ref, qkv_ref, gp_ref, *, tm, halo):
    i = pl.program_id(1)
    has_prev = i > 0
    has_next = i < pl.num_programs(1) - 1
    row = lax.broadcasted_iota(I32, (tm, 1), 0)

    def conv3(cur, prev_row, next_row, w_ref, c0, c1):
        prev_row = jnp.where(has_prev, prev_row, 0.0)
        next_row = jnp.where(has_next, next_row, 0.0)
        up = jnp.where(row == 0, prev_row, pltpu.roll(cur, 1, axis=0))
        dn = jnp.where(row == tm - 1, next_row, pltpu.roll(cur, tm - 1, axis=0))
        return up * w_ref[0:1, c0:c1] + cur * w_ref[1:2, c0:c1] + dn * w_ref[2:3, c0:c1]

    def cols(ref, r0, r1, c0, c1):
        return ref[0, r0:r1, c0:c1].astype(F32)

    cd = CONV_DIM
    cur = cols(p_ref, 0, tm, 2 * cd, 3 * cd) * cols(p_ref, 0, tm, 0, cd)
    prv = cols(pp_ref, halo - 1, halo, 2 * cd, 3 * cd) * cols(pp_ref, halo - 1, halo, 0, cd)
    nxt = cols(pn_ref, 0, 1, 2 * cd, 3 * cd) * cols(pn_ref, 0, 1, 0, cd)
    yc = cols(p_ref, 0, tm, cd, 2 * cd) * conv3(cur, prv, nxt, ca_ref, 0, cd)
    yc_ref[0] = yc.astype(BF16)

    base = 3 * cd
    for part in range(3):
        c0 = base + part * cd
        cur = cols(p_ref, 0, tm, c0, c0 + cd)
        prv = cols(pp_ref, halo - 1, halo, c0, c0 + cd)
        nxt = cols(pn_ref, 0, 1, c0, c0 + cd)
        y = _silu(conv3(cur, prv, nxt, cq_ref, part * cd, (part + 1) * cd))
        if part < 2:
            outs = []
            for hd in range(GDN_HEADS):
                yh = y[:, hd * GDN_DK:(hd + 1) * GDN_DK]
                ss = jnp.sum(yh * yh, axis=-1, keepdims=True)
                yh = yh * lax.rsqrt(ss + EPS)
                if part == 0:
                    yh = yh * (GDN_DK ** -0.5)
                outs.append(yh)
            y = jnp.concatenate(outs, axis=1)
        qkv_ref[0, :, part * cd:(part + 1) * cd] = y.astype(BF16)

    g = g_ref[0]
    lane = lax.broadcasted_iota(I32, g.shape, 1)
    z = g + dt_ref[...]
    softplus = jnp.maximum(z, 0.0) + jnp.log1p(jnp.exp(-jnp.abs(z)))
    decay = -jnp.exp(al_ref[...]) * softplus
    gp_ref[0] = jnp.where(lane < 2 * GDN_HEADS, decay, jax.nn.sigmoid(g))


def _conv_prep(p, gates, conv_a, conv_qkv, alog_vec, dt_vec):
    b, l, n_main = p.shape
    tm = min(512, l)
    halo = 16
    nh = l // halo
    r = tm // halo
    return pl.pallas_call(
        functools.partial(_conv_prep_kernel, tm=tm, halo=halo),
        grid=(b, l // tm),
        in_specs=[pl.BlockSpec((1, tm, n_main), lambda i, j: (i, j, 0)),
                  pl.BlockSpec((1, halo, n_main), lambda i, j: (i, jnp.maximum(j * r - 1, 0), 0)),
                  pl.BlockSpec((1, halo, n_main), lambda i, j: (i, jnp.minimum((j + 1) * r, nh - 1), 0)),
                  pl.BlockSpec((1, tm, LANES), lambda i, j: (i, j, 0)),
                  pl.BlockSpec(conv_a.shape, lambda i, j: (0, 0)),
                  pl.BlockSpec(conv_qkv.shape, lambda i, j: (0, 0)),
                  pl.BlockSpec((1, LANES), lambda i, j: (0, 0)),
                  pl.BlockSpec((1, LANES), lambda i, j: (0, 0))],
        out_specs=[pl.BlockSpec((1, tm, CONV_DIM), lambda i, j: (i, j, 0)),
                   pl.BlockSpec((1, tm, 3 * CONV_DIM), lambda i, j: (i, j, 0)),
                   pl.BlockSpec((1, tm, LANES), lambda i, j: (i, j, 0))],
        out_shape=[jax.ShapeDtypeStruct((b, l, CONV_DIM), BF16),
                   jax.ShapeDtypeStruct((b, l, 3 * CONV_DIM), BF16),
                   jax.ShapeDtypeStruct((b, l, LANES), F32)],
        compiler_params=_cparams(("arbitrary", "arbitrary")),
        name="l0_conv_prep",
    )(p, p, p, gates, conv_a, conv_qkv, alog_vec, dt_vec)


def _gdn_kernel(qkv_ref, gp_ref, s0_ref, o_ref, sout_ref, state, *, rev):
    i = pl.program_id(1)
    n = GDN_GROUP
    c = GDN_CHUNK
    nc = n // c
    dirn = 1 if rev else 0

    @pl.when(i == 0)
    def _():
        state[...] = s0_ref[0]

    ri = lax.broadcasted_iota(I32, (n, n), 0)
    ci = lax.broadcasted_iota(I32, (n, n), 1)
    same = (ri // c) == (ci // c)
    if rev:
        incl = jnp.logical_and(same, ci >= ri)
        strict = jnp.logical_and(same, ci > ri)
    else:
        incl = jnp.logical_and(same, ci <= ri)
        strict = jnp.logical_and(same, ci < ri)
    eye = jnp.where(ri == ci, 1.0, 0.0).astype(F32)

    gp = gp_ref[0]
    gc_all = _dot_exact_lhs(jnp.where(incl, 1.0, 0.0).astype(BF16), gp)
    gt_all = _dot_exact_lhs(jnp.where(same, 1.0, 0.0).astype(BF16), gp)
    gc_rows = gc_all.T

    for hd in range(GDN_HEADS):
        gcol = dirn * GDN_HEADS + hd
        bcol = 2 * GDN_HEADS + gcol
        gc = gc_all[:, gcol:gcol + 1]
        gcr = gc_rows[gcol:gcol + 1, :]
        gt = gt_all[:, gcol:gcol + 1]
        beta = gp[:, bcol:bcol + 1]
        q16 = qkv_ref[0, :, hd * GDN_DK:(hd + 1) * GDN_DK]
        k16 = qkv_ref[0, :, (GDN_HEADS + hd) * GDN_DK:(GDN_HEADS + hd + 1) * GDN_DK]
        v16 = qkv_ref[0, :, (2 * GDN_HEADS + hd) * GDN_DK:(2 * GDN_HEADS + hd + 1) * GDN_DK]
        q = q16.astype(F32)
        k = k16.astype(F32)
        v = v16.astype(F32)

        decay = jnp.where(incl, jnp.exp(jnp.where(incl, gc - gcr, 0.0)), 0.0)
        kb = k * beta
        a = jnp.where(strict, _dot_nt(kb.astype(BF16), k16) * decay, 0.0)
        t = eye - a
        pw = _mm3(a, a)
        for step in range(5):
            t = t + _mm3(t, pw)
            if step < 4:
                pw = _mm3(pw, pw)

        ex = jnp.exp(gc)
        rhs = jnp.concatenate([(v * beta).astype(BF16), (kb * ex).astype(BF16)], axis=1)
        uw = _dot(t.astype(BF16), rhs)
        u = uw[:, :GDN_DK]
        w = uw[:, GDN_DK:].astype(BF16)
        intra = jnp.where(incl, _dot_nt(q16, k16) * decay, 0.0).astype(BF16)
        qd = (q * ex).astype(BF16)
        kd = (k * jnp.exp(gt - gc)).astype(BF16)

        s = state[hd]
        order = range(nc - 1, -1, -1) if rev else range(nc)
        for cc in order:
            r0 = cc * c
            sb = s.astype(BF16)
            vn = u[r0:r0 + c] - _dot(w[r0:r0 + c], sb)
            vn16 = vn.astype(BF16)
            o_c = _dot(qd[r0:r0 + c], sb) + _dot(intra[r0:r0 + c, r0:r0 + c], vn16)
            gl = jnp.exp(gt[r0:r0 + 1, :])
            s = s * gl + _dot_tn(kd[r0:r0 + c], vn16)
            o_ref[0, r0:r0 + c, hd * GDN_DK:(hd + 1) * GDN_DK] = o_c
        state[hd] = s

    @pl.when(i == pl.num_programs(1) - 1)
    def _():
        sout_ref[0] = state[...]


def _gdn_scan(qkv, gp, s0, rev):
    b, l, _ = qkv.shape
    n = GDN_GROUP
    g = l // n
    hdv = GDN_HEADS * GDN_DK
    if rev:
        idx = lambda i, j: (i, g - 1 - j, 0)
    else:
        idx = lambda i, j: (i, j, 0)
    return pl.pallas_call(
        functools.partial(_gdn_kernel, rev=rev),
        grid=(b, g),
        in_specs=[pl.BlockSpec((1, n, 3 * hdv), idx),
                  pl.BlockSpec((1, n, LANES), idx),
                  pl.BlockSpec((1, GDN_HEADS, GDN_DK, GDN_DK), lambda i, j: (i, 0, 0, 0))],
        out_specs=[pl.BlockSpec((1, n, hdv), idx),
                   pl.BlockSpec((1, GDN_HEADS, GDN_DK, GDN_DK), lambda i, j: (i, 0, 0, 0))],
        out_shape=[jax.ShapeDtypeStruct((b, l, hdv), F32),
                   jax.ShapeDtypeStruct((b, GDN_HEADS, GDN_DK, GDN_DK), F32)],
        scratch_shapes=[pltpu.VMEM((GDN_HEADS, GDN_DK, GDN_DK), F32)],
        compiler_params=_cparams(("arbitrary", "arbitrary")),
        name="l0_gdn_bwd" if rev else "l0_gdn_fwd",
    )(qkv, gp, s0)


def _gdn_out_kernel(of_ref, ob_ref, z_ref, yc_ref, x_ref, gate_ref, gn_ref, w_ref, o_ref):
    o = of_ref[0] + ob_ref[0]
    z = z_ref[0].astype(F32)
    parts = []
    for hd in range(GDN_HEADS):
        oh = o[:, hd * GDN_DK:(hd + 1) * GDN_DK]
        ms = jnp.mean(oh * oh, axis=-1, keepdims=True)
        parts.append(oh * lax.rsqrt(ms + EPS) * gn_ref[...] * _silu(z[:, hd * GDN_DK:(hd + 1) * GDN_DK]))
    yg = jnp.concatenate(parts, axis=1).astype(BF16)
    cd = CONV_DIM
    y = _dot(yc_ref[0], w_ref[0:cd, :]) + _dot(yg, w_ref[cd:, :])
    o_ref[0] = x_ref[0] + gate_ref[0] * y


def _gdn_out(o_f, o_b, p, yconv, x, gate, gn, w_out):
    b, l, d = x.shape
    tm = min(512, l)
    hdv = GDN_HEADS * GDN_DK
    zblk = (3 * CONV_DIM + 3 * hdv) // hdv
    row = lambda i, j: (i, j, 0)
    return pl.pallas_call(
        _gdn_out_kernel,
        grid=(b, l // tm),
        in_specs=[pl.BlockSpec((1, tm, hdv), row),
                  pl.BlockSpec((1, tm, hdv), row),
                  pl.BlockSpec((1, tm, hdv), lambda i, j: (i, j, zblk)),
                  pl.BlockSpec((1, tm, CONV_DIM), row),
                  pl.BlockSpec((1, tm, d), row),
                  pl.BlockSpec((1, 1, d), lambda i, j: (i, 0, 0)),
                  pl.BlockSpec((1, GDN_DK), lambda i, j: (0, 0)),
                  pl.BlockSpec(w_out.shape, lambda i, j: (0, 0))],
        out_specs=pl.BlockSpec((1, tm, d), row),
        out_shape=jax.ShapeDtypeStruct((b, l, d), F32),
        compiler_params=_cparams(("arbitrary", "arbitrary")),
        name="l0_out_proj",
    )(o_f, o_b, p, yconv, x, gate, gn, w_out)


def _router_kernel(x_ref, sh_ref, sc_ref, nw_ref, rt_ref, hm3_ref, aff_ref, *, tm):
    hm = _modulate_tile(x_ref[0], nw_ref[...], sh_ref[0], sc_ref[0])
    for cc in range(SUBLANES):
        hm3_ref[0, pl.ds(cc, tm, stride=SUBLANES), :] = hm[:, cc * LANES:(cc + 1) * LANES]
    rh, rl = _split2(rt_ref[...])
    hh, hl = _split2(hm)
    logits = _dot_nt(rh, hh) + _dot_nt(rh, hl) + _dot_nt(rl, hh)
    m = jnp.max(logits, axis=0, keepdims=True)
    e = jnp.exp(logits - m)
    aff_ref[0] = e / jnp.sum(e, axis=0, keepdims=True)


def _router(x, shift, scale, nw, router_t):
    b, l, d = x.shape
    tm = min(512, l)
    return pl.pallas_call(
        functools.partial(_router_kernel, tm=tm),
        grid=(b, l // tm),
        in_specs=[pl.BlockSpec((1, tm, d), lambda i, j: (i, j, 0)),
                  pl.BlockSpec((1, 1, d), lambda i, j: (i, 0, 0)),
                  pl.BlockSpec((1, 1, d), lambda i, j: (i, 0, 0)),
                  pl.BlockSpec((1, d), lambda i, j: (0, 0)),
                  pl.BlockSpec((N_EXPERTS, d), lambda i, j: (0, 0))],
        out_specs=[pl.BlockSpec((1, tm * SUBLANES, LANES), lambda i, j: (i, j, 0)),
                   pl.BlockSpec((1, N_EXPERTS, tm), lambda i, j: (i, 0, j))],
        out_shape=[jax.ShapeDtypeStruct((b, l * SUBLANES, LANES), F32),
                   jax.ShapeDtypeStruct((b, N_EXPERTS, l), F32)],
        compiler_params=_cparams(("arbitrary", "arbitrary")),
        name="moe_router",
    )(x, shift, scale, nw, router_t)


def _topk_kernel(aff_ref, idx_ref, gate_ref, *, cap, capp):
    r = TOPK_ROWS
    a3 = aff_ref[0]
    bits3 = pltpu.bitcast(a3, I32)

    def count_ge(cand):
        hit = jnp.where(bits3 >= cand, 1.0, 0.0)
        return jnp.sum(jnp.sum(hit, axis=1, keepdims=True), axis=2, keepdims=True)

    thr = jnp.zeros((N_EXPERTS, 1, 1), I32)
    for bit in range(30, -1, -1):
        cand = thr | (1 << bit)
        thr = jnp.where(count_ge(cand) >= cap, cand, thr)

    li = lax.broadcasted_iota(I32, (LANES, LANES), 0)
    lj = lax.broadcasted_iota(I32, (LANES, LANES), 1)
    tri_incl = jnp.where(li <= lj, 1.0, 0.0).astype(BF16)
    tri_strict_rows = jnp.where(lj < li, 1.0, 0.0).astype(BF16)
    ones8 = jnp.ones((SUBLANES, LANES), BF16)
    lane_row = lax.broadcasted_iota(I32, (capp, LANES), 1).astype(F32)
    slot = lax.broadcasted_iota(I32, (capp, 1), 0).astype(F32)

    for ex in range(N_EXPERTS):
        a = aff_ref[0, ex]
        bits = pltpu.bitcast(a, I32)
        th = thr[ex]
        gt = bits > th
        eq = bits == th
        n_gt = jnp.sum(jnp.sum(jnp.where(gt, 1.0, 0.0), axis=0, keepdims=True), axis=1, keepdims=True)
        need = cap - n_gt
        eq16 = jnp.where(eq, 1.0, 0.0).astype(BF16)
        eq_lc = _dot(eq16, tri_incl)
        eq_cnt = jnp.broadcast_to(eq_lc[:, LANES - 1:LANES], (r, LANES)).astype(BF16)
        eq_off = _dot(tri_strict_rows, eq_cnt)
        take_eq = jnp.logical_and(eq, (eq_off + eq_lc) <= need)
        sel16 = jnp.where(jnp.logical_or(gt, take_eq), 1.0, 0.0).astype(BF16)

        lc = _dot(sel16, tri_incl)
        cnt_row = _dot_nt(ones8, sel16)
        off_incl = _dot(cnt_row.astype(BF16), tri_incl)
        off_excl = off_incl - cnt_row
        jrow = jnp.sum(jnp.where(off_incl[0:1, :] <= slot, 1.0, 0.0), axis=1, keepdims=True)
        onehot = lane_row == jrow
        onehot16 = jnp.where(onehot, 1.0, 0.0).astype(BF16)
        lc_rows = _dot(onehot16, lc.astype(BF16))
        off_s = jnp.sum(jnp.where(onehot, off_excl[0:1, :], 0.0), axis=1, keepdims=True)
        rank = slot - off_s
        pos = jnp.sum(jnp.where(lc_rows <= rank, 1.0, 0.0), axis=1, keepdims=True)
        tok = jrow * float(LANES) + pos
        ah, am, al = _split3(a)
        a_rows = _dot(onehot16, ah) + _dot(onehot16, am) + _dot(onehot16, al)
        gate = jnp.sum(jnp.where(lane_row == pos, a_rows, 0.0), axis=1, keepdims=True)
        tok_t = jnp.broadcast_to(tok, (capp, LANES)).T
        gate_t = jnp.broadcast_to(gate, (capp, LANES)).T
        idx_ref[0, ex] = tok_t[0:1, :].astype(I32)
        gate_ref[0, ex] = gate_t[0:1, :]


def _topk(aff_t, cap):
    b, e, n = aff_t.shape
    npad = TOPK_ROWS * LANES
    capp = max(cap, LANES)
    a = jnp.pad(aff_t, ((0, 0), (0, 0), (0, npad - n)), constant_values=-1.0).reshape(b, e, TOPK_ROWS, LANES)
    idx, gate = pl.pallas_call(
        functools.partial(_topk_kernel, cap=cap, capp=capp),
        grid=(b,),
        in_specs=[pl.BlockSpec((1, e, TOPK_ROWS, LANES), lambda i: (i, 0, 0, 0))],
        out_specs=[pl.BlockSpec((1, e, 1, capp), lambda i: (i, 0, 0, 0)),
                   pl.BlockSpec((1, e, 1, capp), lambda i: (i, 0, 0, 0))],
        out_shape=[jax.ShapeDtypeStruct((b, e, 1, capp), I32),
                   jax.ShapeDtypeStruct((b, e, 1, capp), F32)],
        compiler_params=_cparams(("arbitrary",)),
        name="moe_topk",
    )(a)
    return idx[:, :, 0, :cap], gate[:, :, 0, :cap]


def _gather_kernel(idx_ref, hm3_hbm, xe_ref, stage, sem, *, cap):
    b = pl.program_id(0)

    def row_copy(s):
        t = idx_ref[0, 0, s]
        src = hm3_hbm.at[b, pl.ds(pl.multiple_of(t * SUBLANES, SUBLANES), SUBLANES), :]
        dst = stage.at[pl.ds(pl.multiple_of(s * SUBLANES, SUBLANES), SUBLANES), :]
        return pltpu.make_async_copy(src, dst, sem)

    def issue(g, carry):
        for u in range(SUBLANES):
            row_copy(g * SUBLANES + u).start()
        return carry

    lax.fori_loop(0, cap // SUBLANES, issue, 0)
    pltpu.make_async_copy(hm3_hbm.at[b, pl.ds(0, cap * SUBLANES), :], stage, sem).wait()
    for cc in range(SUBLANES):
        xe_ref[0, 0, :, cc * LANES:(cc + 1) * LANES] = stage[pl.ds(cc, cap, stride=SUBLANES), :].astype(BF16)


def _gather(idx, hm3, d):
    b, e, cap = idx.shape
    return pl.pallas_call(
        functools.partial(_gather_kernel, cap=cap),
        grid=(b, e),
        in_specs=[pl.BlockSpec((1, 1, cap), lambda i, j: (i * e + j, 0, 0), memory_space=pltpu.SMEM),
                  pl.BlockSpec(memory_space=pl.ANY)],
        out_specs=pl.BlockSpec((1, 1, cap, d), lambda i, j: (i, j, 0, 0)),
        out_shape=jax.ShapeDtypeStruct((b, e, cap, d), BF16),
        scratch_shapes=[pltpu.VMEM((cap * SUBLANES, LANES), F32), pltpu.SemaphoreType.DMA(())],
        compiler_params=_cparams(("arbitrary", "arbitrary")),
        name="moe_gather",
    )(idx.reshape(b * e, 1, cap), hm3)


def _ffn_kernel(x_ref, wg_ref, wu_ref, wd_ref, y3_ref, acc, *, cap):
    f = pl.program_id(2)
    x = x_ref[0, 0]
    g = _dot(x, wg_ref[0].astype(BF16))
    u = _dot(x, wu_ref[0].astype(BF16))
    h = (_silu(g) * u).astype(BF16)
    part = _dot(h, wd_ref[0].astype(BF16))

    @pl.when(f == 0)
    def _():
        acc[...] = part

    @pl.when(f > 0)
    def _():
        acc[...] += part

    @pl.when(f == pl.num_programs(2) - 1)
    def _():
        for cc in range(SUBLANES):
            y3_ref[0, 0, pl.ds(cc, cap, stride=SUBLANES), :] = acc[:, cc * LANES:(cc + 1) * LANES]


def _ffn(xe, wg, wu, wd):
    b, e, cap, d = xe.shape
    ff = wg.shape[2]
    tf = 512
    return pl.pallas_call(
        functools.partial(_ffn_kernel, cap=cap),
        grid=(b, e, ff // tf),
        in_specs=[pl.BlockSpec((1, 1, cap, d), lambda i, j, k: (i, j, 0, 0)),
                  pl.BlockSpec((1, d, tf), lambda i, j, k: (j, 0, k)),
                  pl.BlockSpec((1, d, tf), lambda i, j, k: (j, 0, k)),
                  pl.BlockSpec((1, tf, d), lambda i, j, k: (j, k, 0))],
        out_specs=pl.BlockSpec((1, 1, cap * SUBLANES, LANES), lambda i, j, k: (i, j, 0, 0)),
        out_shape=jax.ShapeDtypeStruct((b, e, cap * SUBLANES, LANES), F32),
        scratch_shapes=[pltpu.VMEM((cap, d), F32)],
        compiler_params=_cparams(("arbitrary", "arbitrary", "arbitrary")),
        name="moe_ffn",
    )(xe, wg, wu, wd)


def _combine_kernel(idx_ref, gate_ref, y3_ref, out_hbm, acc, sem, *, cap, unroll):
    b = pl.program_id(0)
    e = pl.program_id(1)

    @pl.when(e == 0)
    def _():
        acc[...] = jnp.zeros_like(acc)

    def body(g, carry):
        base = g * unroll
        rows = []
        for u in range(unroll):
            s = base + u
            t = pl.multiple_of(idx_ref[0, 0, s] * SUBLANES, SUBLANES)
            y = y3_ref[0, 0, pl.ds(pl.multiple_of(s * SUBLANES, SUBLANES), SUBLANES), :]
            rows.append((t, acc[pl.ds(t, SUBLANES), :] + gate_ref[0, 0, s] * y))
        for t, val in rows:
            acc[pl.ds(t, SUBLANES), :] = val
        return carry

    lax.fori_loop(0, cap // unroll, body, 0)

    @pl.when(e == pl.num_programs(1) - 1)
    def _():
        cp = pltpu.make_async_copy(acc, out_hbm.at[b], sem)
        cp.start()
        cp.wait()


def _combine(idx, gate, y3, l):
    b, e, cap = idx.shape
    return pl.pallas_call(
        functools.partial(_combine_kernel, cap=cap, unroll=4),
        grid=(b, e),
        in_specs=[pl.BlockSpec((1, 1, cap), lambda i, j: (i * e + j, 0, 0), memory_space=pltpu.SMEM),
                  pl.BlockSpec((1, 1, cap), lambda i, j: (i * e + j, 0, 0), memory_space=pltpu.SMEM),
                  pl.BlockSpec((1, 1, cap * SUBLANES, LANES), lambda i, j: (i, j, 0, 0))],
        out_specs=pl.BlockSpec(memory_space=pl.ANY),
        out_shape=jax.ShapeDtypeStruct((b, l * SUBLANES, LANES), F32),
        scratch_shapes=[pltpu.VMEM((l * SUBLANES, LANES), F32), pltpu.SemaphoreType.DMA(())],
        compiler_params=_cparams(("arbitrary", "arbitrary")),
        name="moe_combine",
    )(idx.reshape(b * e, 1, cap), gate.reshape(b * e, 1, cap), y3)


def _moe_resid_kernel(x_ref, acc3_ref, gate_ref, fw_ref, o_ref, *, tm, final):
    parts = [acc3_ref[0, pl.ds(cc, tm, stride=SUBLANES), :] for cc in range(SUBLANES)]
    y = x_ref[0] + gate_ref[0] * jnp.concatenate(parts, axis=1)
    if final:
        ms = jnp.mean(y * y, axis=-1, keepdims=True)
        y = y * lax.rsqrt(ms + EPS) * fw_ref[...]
    o_ref[0] = y


def _moe_resid(x, acc3, gate, final_w, final):
    b, l, d = x.shape
    tm = min(512, l)
    return pl.pallas_call(
        functools.partial(_moe_resid_kernel, tm=tm, final=final),
        grid=(b, l // tm),
        in_specs=[pl.BlockSpec((1, tm, d), lambda i, j: (i, j, 0)),
                  pl.BlockSpec((1, tm * SUBLANES, LANES), lambda i, j: (i, j, 0)),
                  pl.BlockSpec((1, 1, d), lambda i, j: (i, 0, 0)),
                  pl.BlockSpec((1, d), lambda i, j: (0, 0))],
        out_specs=pl.BlockSpec((1, tm, d), lambda i, j: (i, j, 0)),
        out_shape=jax.ShapeDtypeStruct((b, l, d), F32),
        compiler_params=_cparams(("arbitrary", "arbitrary")),
        name="moe_resid",
    )(x, acc3, gate, final_w)


def _moe(x, shift, scale, gate, nw, router_t, wg, wu, wd, final_w, final):
    b, l, d = x.shape
    cap = CAPACITY_FACTOR * l // N_EXPERTS
    hm3, aff_t = _router(x, shift, scale, nw, router_t)
    idx, gates = _topk(aff_t, cap)
    xe = _gather(idx, hm3, d)
    y3 = _ffn(xe, wg, wu, wd)
    acc3 = _combine(idx, gates, y3, l)
    return _moe_resid(x, acc3, gate, final_w, final)


def _qkv_kernel(x_ref, sh_ref, sc_ref, nw_ref, w_ref, cos_ref, sin_ref, q_ref, k_ref, vt_ref, *, rope, qscale):
    h = _modulate_tile(x_ref[0], nw_ref[...], sh_ref[0], sc_ref[0]).astype(BF16)
    dq = q_ref.shape[2]
    chunk = 512
    rep = chunk // LANES
    if rope:
        cos = jnp.concatenate([cos_ref[...]] * rep, axis=1)
        sin = jnp.concatenate([sin_ref[...]] * rep, axis=1)
        lane = lax.broadcasted_iota(I32, (x_ref.shape[1], chunk), 1)
        first = (lane % 32) < 16
    hd = 2 * DIFF_DH
    for which, ref in enumerate((q_ref, k_ref, vt_ref)):
        for c0 in range(0, dq, chunk):
            y = _dot(h, w_ref[:, which * dq + c0:which * dq + c0 + chunk])
            if rope and which < 2:
                partner = jnp.where(first, pltpu.roll(y, chunk - 16, axis=1), pltpu.roll(y, 16, axis=1))
                y = y * cos + partner * sin
            if which == 0:
                y = y * qscale
            if which < 2:
                ref[0, :, c0:c0 + chunk] = y.astype(BF16)
            else:
                for c1 in range(0, chunk, hd):
                    ref[0, c0 + c1:c0 + c1 + hd, :] = y[:, c1:c1 + hd].T.astype(BF16)


def _qkv_proj(x, shift, scale, nw, w, cos_t, sin_t, rope):
    b, l, d = x.shape
    tm = min(512, l)
    dq = w.shape[1] // 3
    qscale = (DIFF_DH ** -0.5) * LOG2E
    row = lambda i, j: (i, j, 0)
    out = jax.ShapeDtypeStruct((b, l, dq), BF16)
    return pl.pallas_call(
        functools.partial(_qkv_kernel, rope=rope, qscale=qscale),
        grid=(b, l // tm),
        in_specs=[pl.BlockSpec((1, tm, d), row),
                  pl.BlockSpec((1, 1, d), lambda i, j: (i, 0, 0)),
                  pl.BlockSpec((1, 1, d), lambda i, j: (i, 0, 0)),
                  pl.BlockSpec((1, d), lambda i, j: (0, 0)),
                  pl.BlockSpec(w.shape, lambda i, j: (0, 0)),
                  pl.BlockSpec((tm, LANES), lambda i, j: (j, 0)),
                  pl.BlockSpec((tm, LANES), lambda i, j: (j, 0))],
        out_specs=[pl.BlockSpec((1, tm, dq), row), pl.BlockSpec((1, tm, dq), row),
                   pl.BlockSpec((1, dq, tm), lambda i, j: (i, 0, j))],
        out_shape=[out, out, jax.ShapeDtypeStruct((b, dq, l), BF16)],
        compiler_params=_cparams(("arbitrary", "arbitrary")),
        name="l1_qkv_rope" if rope else "l1_qkv_ctx",
    )(x, shift, scale, nw, w, cos_t, sin_t)


def _rope_tables(l):
    t = np.arange(l)
    n_freq = DIFF_DH // 4
    inv_freq = ROPE_BASE ** (-np.arange(n_freq, dtype=np.float32) / n_freq)
    ang_row = (t // GRID_W).astype(np.float32)[:, None] * inv_freq
    ang_col = (t % GRID_W).astype(np.float32)[:, None] * inv_freq
    ang = np.concatenate([ang_row, ang_row, ang_col, ang_col], axis=1)
    sign = np.concatenate([-np.ones(n_freq), np.ones(n_freq)] * 2).astype(np.float32)
    cos = np.cos(ang).astype(np.float32)
    sin = (np.sin(ang) * sign).astype(np.float32)
    return jnp.asarray(np.concatenate([cos, cos], axis=1)), jnp.asarray(np.concatenate([sin, sin], axis=1))


def _attn_kernel(q_ref, k_ref, vt_ref, lam_ref, sub_ref, o_ref, *, tq, tk, lam_init):
    lq = q_ref.shape[1]
    lk = k_ref.shape[1]
    dv = vt_ref.shape[1]
    nk = lk // tk
    lp = lam_ref[...]
    lam = (jnp.exp(jnp.sum(lp[0:1] * lp[1:2], axis=1, keepdims=True))
           - jnp.exp(jnp.sum(lp[2:3] * lp[3:4], axis=1, keepdims=True)) + lam_init)

    def update(s, vt, m, l, acc):
        mn = jnp.maximum(m, jnp.max(s, axis=0, keepdims=True))
        alpha = jnp.exp2(m - mn)
        p = jnp.exp2(s - mn)
        l = alpha * l + jnp.sum(p, axis=0, keepdims=True)
        acc = alpha * acc + _dot(vt, p.astype(BF16))
        return mn, l, acc

    def q_tile(i, carry):
        r0 = pl.multiple_of(i * tq, tq)
        q = q_ref[0, pl.ds(r0, tq), :]
        lane = lax.broadcasted_iota(I32, q.shape, 1)
        zero = jnp.zeros_like(q)
        q1 = jnp.where(lane < DIFF_DH, q, zero)
        q2 = jnp.where(lane >= DIFF_DH, q, zero)
        m1 = m2 = jnp.full((1, tq), -1e30, F32)
        l1 = l2 = jnp.zeros((1, tq), F32)
        a1 = a2 = jnp.zeros((dv, tq), F32)
        s1 = _dot_nt(k_ref[0, 0:tk, :], q1)
        s2 = _dot_nt(k_ref[0, 0:tk, :], q2)
        for j in range(nk):
            vt = vt_ref[0, :, j * tk:(j + 1) * tk]
            if j + 1 < nk:
                kn = k_ref[0, (j + 1) * tk:(j + 2) * tk, :]
                s1n = _dot_nt(kn, q1)
            m1, l1, a1 = update(s1, vt, m1, l1, a1)
            if j + 1 < nk:
                s2n = _dot_nt(kn, q2)
            m2, l2, a2 = update(s2, vt, m2, l2, a2)
            if j + 1 < nk:
                s1, s2 = s1n, s2n
        o = a1 / l1 - lam * (a2 / l2)
        ms = jnp.mean(o * o, axis=0, keepdims=True)
        o = (o * lax.rsqrt(ms + EPS)).T * (sub_ref[...] * (1.0 - lam_init))
        o_ref[0, pl.ds(r0, tq), :] = o.astype(BF16)
        return carry

    lax.fori_loop(0, lq // tq, q_tile, 0)


def _attention(q, k_all, vt_all, lam_params, subln, lam_init):
    b, lq, dq = q.shape
    lk = k_all.shape[1]
    hd = 2 * DIFF_DH
    tq = min(256, lq)
    tk = 768 if lk % 768 == 0 else 256
    return pl.pallas_call(
        functools.partial(_attn_kernel, tq=tq, tk=tk, lam_init=lam_init),
        grid=(b, DIFF_HEADS),
        in_specs=[pl.BlockSpec((1, lq, hd), lambda i, h: (i, 0, h)),
                  pl.BlockSpec((1, lk, hd), lambda i, h: (i, 0, h)),
                  pl.BlockSpec((1, hd, lk), lambda i, h: (i, h, 0)),
                  pl.BlockSpec(lam_params.shape, lambda i, h: (0, 0)),
                  pl.BlockSpec((1, hd), lambda i, h: (0, 0))],
        out_specs=pl.BlockSpec((1, lq, hd), lambda i, h: (i, 0, h)),
        out_shape=jax.ShapeDtypeStruct((b, lq, dq), BF16),
        compiler_params=_cparams(("arbitrary", "arbitrary")),
        name="l1_diff_attn",
    )(q, k_all, vt_all, lam_params, subln)


def _resid_matmul_kernel(a_ref, w_ref, x_ref, gate_ref, o_ref):
    o_ref[0] = x_ref[0] + gate_ref[0] * _dot(a_ref[0], w_ref[...])


def _resid_matmul(a, w, x, gate):
    b, l, d = x.shape
    tm = min(512, l)
    row = lambda i, j: (i, j, 0)
    return pl.pallas_call(
        _resid_matmul_kernel,
        grid=(b, l // tm),
        in_specs=[pl.BlockSpec((1, tm, a.shape[2]), row),
                  pl.BlockSpec(w.shape, lambda i, j: (0, 0)),
                  pl.BlockSpec((1, tm, d), row),
                  pl.BlockSpec((1, 1, d), lambda i, j: (i, 0, 0))],
        out_specs=pl.BlockSpec((1, tm, d), row),
        out_shape=jax.ShapeDtypeStruct((b, l, d), F32),
        compiler_params=_cparams(("arbitrary", "arbitrary")),
        name="l1_out_proj",
    )(a, w, x, gate)


def _mod_vectors(c, c_ctx, w, b):
    bn, d = c.shape
    cond = jnp.concatenate([c, c_ctx[None], jnp.zeros((8 - bn - 1, d), F32)], axis=0)
    m = _adaln(cond, w, b)
    lat = [m[:bn, i * d:(i + 1) * d][:, None, :] for i in range(6)]
    ctx = [jnp.broadcast_to(m[bn:bn + 1, i * d:(i + 1) * d][:, None, :], (bn, 1, d)) for i in range(6)]
    return lat, ctx


def _row(v):
    return v.reshape(1, -1)


def kernel(x, c, ctx, c_ctx, l0_mod_w, l0_mod_b, l0_norm1, l0_w_in, l0_conv_a, l0_conv_qkv, l0_a_log, l0_dt_bias, l0_gdn_norm, l0_w_out, l0_norm2, l0_router, l0_exp_gate, l0_exp_up, l0_exp_down, l1_mod_w, l1_mod_b, l1_norm1, l1_w_qkv, l1_lambda_q1, l1_lambda_k1, l1_lambda_q2, l1_lambda_k2, l1_subln, l1_w_o, l1_norm2, l1_router, l1_exp_gate, l1_exp_up, l1_exp_down, final_norm):
    bn, seq, d = x.shape
    n_main = 3 * CONV_DIM + 4 * GDN_HEADS * GDN_DK
    n_gate = 4 * GDN_HEADS

    m, mc = _mod_vectors(c, c_ctx, l0_mod_w, l0_mod_b)
    w_in = jnp.concatenate([l0_w_in, jnp.zeros((d, LANES - n_gate), F32)], axis=1).astype(BF16)
    w_out = l0_w_out.astype(BF16)
    pad = jnp.zeros((LANES - 2 * GDN_HEADS,), F32)
    alog_vec = jnp.concatenate([l0_a_log.reshape(-1), pad]).reshape(1, LANES)
    dt_vec = jnp.concatenate([l0_dt_bias.reshape(-1), pad]).reshape(1, LANES)
    router0_t = l0_router.T

    def mixer0(tok, mod, s0_f, s0_b):
        p, gates = _in_proj(tok, mod[0], mod[1], _row(l0_norm1), w_in, n_main)
        yconv, qkv, gp = _conv_prep(p, gates, l0_conv_a, l0_conv_qkv, alog_vec, dt_vec)
        o_f, s_f = _gdn_scan(qkv, gp, s0_f, rev=False)
        o_b, s_b = _gdn_scan(qkv, gp, s0_b, rev=True)
        out = _gdn_out(o_f, o_b, p, yconv, tok, mod[2], _row(l0_gdn_norm), w_out)
        return out, s_f, s_b

    s0 = jnp.zeros((bn, GDN_HEADS, GDN_DK, GDN_DK), F32)
    ctx1, s_f, s_b = mixer0(ctx, mc, s0, s0)
    x1, _, _ = mixer0(x, m, s_f, s_b)
    moe0 = functools.partial(_moe, nw=_row(l0_norm2), router_t=router0_t, wg=l0_exp_gate, wu=l0_exp_up,
                             wd=l0_exp_down, final_w=_row(final_norm), final=False)
    x2 = moe0(x1, m[3], m[4], m[5])
    ctx2 = moe0(ctx1, mc[3], mc[4], mc[5])

    m, mc = _mod_vectors(c, c_ctx, l1_mod_w, l1_mod_b)
    lam_init = 0.8 - 0.6 * math.exp(-0.3 * 1)
    w_qkv = l1_w_qkv.astype(BF16)
    cos_t, sin_t = _rope_tables(seq)
    q, k_lat, vt_lat = _qkv_proj(x2, m[0], m[1], _row(l1_norm1), w_qkv, cos_t, sin_t, rope=True)
    lc = ctx2.shape[1]
    _, k_ctx, vt_ctx = _qkv_proj(ctx2, mc[0], mc[1], _row(l1_norm1), w_qkv, cos_t[:lc], sin_t[:lc], rope=False)
    k_all = jnp.concatenate([k_ctx, k_lat], axis=1)
    vt_all = jnp.concatenate([vt_ctx, vt_lat], axis=2)
    lam_params = jnp.stack([l1_lambda_q1, l1_lambda_k1, l1_lambda_q2, l1_lambda_k2])
    o = _attention(q, k_all, vt_all, lam_params, _row(l1_subln), lam_init)
    x3 = _resid_matmul(o, l1_w_o.astype(BF16), x2, m[2])
    return _moe(x3, m[3], m[4], m[5], _row(l1_norm2), l1_router.T, l1_exp_gate, l1_exp_up, l1_exp_down,
                _row(final_norm), True)
```

```python
import functools
import math

import jax
import jax.numpy as jnp
import numpy as np
from jax import lax
from jax.experimental import pallas as pl
from jax.experimental.pallas import tpu as pltpu

F32 = jnp.float32
BF16 = jnp.bfloat16
I32 = jnp.int32

EPS = 1e-6
GRID_W = 64
ROPE_BASE = 10000.0
CONV_DIM = 512
GDN_HEADS = 4
GDN_DK = 128
GDN_CHUNK = 64
GDN_GROUP = 256
DIFF_HEADS = 8
DIFF_DH = 64
N_EXPERTS = 16
CAPACITY_FACTOR = 2
LANES = 128
SUBLANES = 8
TOPK_ROWS = 128
V7X_VMEM_LIMIT = 56 * 1024 * 1024
LOG2E = 1.4426950408889634


def _cparams(sem, vmem=None):
    return pltpu.CompilerParams(dimension_semantics=sem, vmem_limit_bytes=vmem or V7X_VMEM_LIMIT)


def _dot(a, b):
    return jnp.dot(a, b, preferred_element_type=F32)


def _dot_nt(a, b):
    return lax.dot_general(a, b, (((1,), (1,)), ((), ())), preferred_element_type=F32)


def _dot_tn(a, b):
    return lax.dot_general(a, b, (((0,), (0,)), ((), ())), preferred_element_type=F32)


def _split2(x):
    hi = x.astype(BF16)
    lo = (x - hi.astype(F32)).astype(BF16)
    return hi, lo


def _split3(x):
    hi = x.astype(BF16)
    r = x - hi.astype(F32)
    mid = r.astype(BF16)
    lo = (r - mid.astype(F32)).astype(BF16)
    return hi, mid, lo


def _dot_exact_lhs(m_bf16, x):
    hi, mid, lo = _split3(x)
    return _dot(m_bf16, hi) + _dot(m_bf16, mid) + _dot(m_bf16, lo)


def _mm3(a, b):
    ah, al = _split2(a)
    bh, bl = _split2(b)
    return _dot(ah, bh) + _dot(ah, bl) + _dot(al, bh)


def _silu(x):
    return x * jax.nn.sigmoid(x)


def _modulate_tile(x, nw, shift, scale):
    ms = jnp.mean(x * x, axis=-1, keepdims=True)
    return (x * lax.rsqrt(ms + EPS)) * nw * (1.0 + scale) + shift


def _adaln_kernel(c_ref, w_ref, b_ref, o_ref):
    s = _silu(c_ref[...])
    o_ref[...] = _dot(s.astype(BF16), w_ref[...].astype(BF16)) + b_ref[...]


def _adaln(cond8, w, b):
    d, n = w.shape
    tn = 1536
    return pl.pallas_call(
        _adaln_kernel,
        grid=(n // tn,),
        in_specs=[pl.BlockSpec((8, d), lambda j: (0, 0)),
                  pl.BlockSpec((d, tn), lambda j: (0, j)),
                  pl.BlockSpec((1, tn), lambda j: (0, j))],
        out_specs=pl.BlockSpec((8, tn), lambda j: (0, j)),
        out_shape=jax.ShapeDtypeStruct((8, n), F32),
        compiler_params=_cparams(("arbitrary",)),
        name="adaln",
    )(cond8, w, b.reshape(1, n))


def _in_proj_kernel(x_ref, sh_ref, sc_ref, nw_ref, w_ref, p_ref, g_ref, *, n_main, chunk):
    h = _modulate_tile(x_ref[0], nw_ref[...], sh_ref[0], sc_ref[0]).astype(BF16)
    for n0 in range(0, n_main, chunk):
        p_ref[0, :, n0:n0 + chunk] = _dot(h, w_ref[:, n0:n0 + chunk]).astype(BF16)
    g_ref[0] = _dot(h, w_ref[:, n_main:])


def _in_proj(x, shift, scale, nw, w_pad, n_main):
    b, l, d = x.shape
    tm = min(512, l)
    n_all = w_pad.shape[1]
    return pl.pallas_call(
        functools.partial(_in_proj_kernel, n_main=n_main, chunk=512),
        grid=(b, l // tm),
        in_specs=[pl.BlockSpec((1, tm, d), lambda i, j: (i, j, 0)),
                  pl.BlockSpec((1, 1, d), lambda i, j: (i, 0, 0)),
                  pl.BlockSpec((1, 1, d), lambda i, j: (i, 0, 0)),
                  pl.BlockSpec((1, d), lambda i, j: (0, 0)),
                  pl.BlockSpec((d, n_all), lambda i, j: (0, 0))],
        out_specs=[pl.BlockSpec((1, tm, n_main), lambda i, j: (i, j, 0)),
                   pl.BlockSpec((1, tm, LANES), lambda i, j: (i, j, 0))],
        out_shape=[jax.ShapeDtypeStruct((b, l, n_main), BF16),
                   jax.ShapeDtypeStruct((b, l, LANES), F32)],
        compiler_params=_cparams(("arbitrary", "arbitrary")),
        name="l0_in_proj",
    )(x, shift, scale, nw, w_pad)


def _conv_prep_kernel(p_ref, pp_ref, pn_ref, g_ref, ca_ref, cq_ref, al_ref, dt_ref,
                      yc_ref, qkv_ref, gp_ref, *, tm, halo):
    i = pl.program_id(1)
    has_prev = i > 0
    has_next = i < pl.num_programs(1) - 1
    row = lax.broadcasted_iota(I32, (tm, 1), 0)

    def conv3(cur, prev_row, next_row, w_ref, c0, c1):
        prev_row = jnp.where(has_prev, prev_row, 0.0)
        next_row = jnp.where(has_next, next_row, 0.0)
        up = jnp.where(row == 0, prev_row, pltpu.roll(cur, 1, axis=0))
        dn = jnp.where(row == tm - 1, next_row, pltpu.roll(cur, tm - 1, axis=0))
        return up * w_ref[0:1, c0:c1] + cur * w_ref[1:2, c0:c1] + dn * w_ref[2:3, c0:c1]

    def cols(ref, r0, r1, c0, c1):
        return ref[0, r0:r1, c0:c1].astype(F32)

    cd = CONV_DIM
    cur = cols(p_ref, 0, tm, 2 * cd, 3 * cd) * cols(p_ref, 0, tm, 0, cd)
    prv = cols(pp_ref, halo - 1, halo, 2 * cd, 3 * cd) * cols(pp_ref, halo - 1, halo, 0, cd)
    nxt = cols(pn_ref, 0, 1, 2 * cd, 3 * cd) * cols(pn_ref, 0, 1, 0, cd)
    yc = cols(p_ref, 0, tm, cd, 2 * cd) * conv3(cur, prv, nxt, ca_ref, 0, cd)
    yc_ref[0] = yc.astype(BF16)

    base = 3 * cd
    for part in range(3):
        c0 = base + part * cd
        cur = cols(p_ref, 0, tm, c0, c0 + cd)
        prv = cols(pp_ref, halo - 1, halo, c0, c0 + cd)
        nxt = cols(pn_ref, 0, 1, c0, c0 + cd)
        y = _silu(conv3(cur, prv, nxt, cq_ref, part * cd, (part + 1) * cd))
        if part < 2:
            outs = []
            for hd in range(GDN_HEADS):
                yh = y[:, hd * GDN_DK:(hd + 1) * GDN_DK]
                ss = jnp.sum(yh * yh, axis=-1, keepdims=True)
                yh = yh * lax.rsqrt(ss + EPS)
                if part == 0:
                    yh = yh * (GDN_DK ** -0.5)
                outs.append(yh)
            y = jnp.concatenate(outs, axis=1)
        qkv_ref[0, :, part * cd:(part + 1) * cd] = y.astype(BF16)

    g = g_ref[0]
    lane = lax.broadcasted_iota(I32, g.shape, 1)
    z = g + dt_ref[...]
    softplus = jnp.maximum(z, 0.0) + jnp.log1p(jnp.exp(-jnp.abs(z)))
    decay = -jnp.exp(al_ref[...]) * softplus
    gp_ref[0] = jnp.where(lane < 2 * GDN_HEADS, decay, jax.nn.sigmoid(g))


def _conv_prep(p, gates, conv_a, conv_qkv, alog_vec, dt_vec):
    b, l, n_main = p.shape
    tm = min(512, l)
    halo = 16
    nh = l // halo
    r = tm // halo
    return pl.pallas_call(
        functools.partial(_conv_prep_kernel, tm=tm, halo=halo),
        grid=(b, l // tm),
        in_specs=[pl.BlockSpec((1, tm, n_main), lambda i, j: (i, j, 0)),
                  pl.BlockSpec((1, halo, n_main), lambda i, j: (i, jnp.maximum(j * r - 1, 0), 0)),
                  pl.BlockSpec((1, halo, n_main), lambda i, j: (i, jnp.minimum((j + 1) * r, nh - 1), 0)),
                  pl.BlockSpec((1, tm, LANES), lambda i, j: (i, j, 0)),
                  pl.BlockSpec(conv_a.shape, lambda i, j: (0, 0)),
                  pl.BlockSpec(conv_qkv.shape, lambda i, j: (0, 0)),
                  pl.BlockSpec((1, LANES), lambda i, j: (0, 0)),
                  pl.BlockSpec((1, LANES), lambda i, j: (0, 0))],
        out_specs=[pl.BlockSpec((1, tm, CONV_DIM), lambda i, j: (i, j, 0)),
                   pl.BlockSpec((1, tm, 3 * CONV_DIM), lambda i, j: (i, j, 0)),
                   pl.BlockSpec((1, tm, LANES), lambda i, j: (i, j, 0))],
        out_shape=[jax.ShapeDtypeStruct((b, l, CONV_DIM), BF16),
                   jax.ShapeDtypeStruct((b, l, 3 * CONV_DIM), BF16),
                   jax.ShapeDtypeStruct((b, l, LANES), F32)],
        compiler_params=_cparams(("arbitrary", "arbitrary")),
        name="l0_conv_prep",
    )(p, p, p, gates, conv_a, conv_qkv, alog_vec, dt_vec)


def _gdn_kernel(qkv_ref, gp_ref, s0_ref, o_ref, sout_ref, state, *, rev):
    i = pl.program_id(1)
    n = GDN_GROUP
    c = GDN_CHUNK
    nc = n // c
    dirn = 1 if rev else 0

    @pl.when(i == 0)
    def _():
        state[...] = s0_ref[0]

    ri = lax.broadcasted_iota(I32, (n, n), 0)
    ci = lax.broadcasted_iota(I32, (n, n), 1)
    same = (ri // c) == (ci // c)
    if rev:
        incl = jnp.logical_and(same, ci >= ri)
        strict = jnp.logical_and(same, ci > ri)
    else:
        incl = jnp.logical_and(same, ci <= ri)
        strict = jnp.logical_and(same, ci < ri)
    eye = jnp.where(ri == ci, 1.0, 0.0).astype(F32)

    gp = gp_ref[0]
    gc_all = _dot_exact_lhs(jnp.where(incl, 1.0, 0.0).astype(BF16), gp)
    gt_all = _dot_exact_lhs(jnp.where(same, 1.0, 0.0).astype(BF16), gp)
    gc_rows = gc_all.T

    heads = range(GDN_HEADS)
    hv = []
    for hd in heads:
        gcol = dirn * GDN_HEADS + hd
        bcol = 2 * GDN_HEADS + gcol
        gc = gc_all[:, gcol:gcol + 1]
        gcr = gc_rows[gcol:gcol + 1, :]
        gt = gt_all[:, gcol:gcol + 1]
        beta = gp[:, bcol:bcol + 1]
        q16 = qkv_ref[0, :, hd * GDN_DK:(hd + 1) * GDN_DK]
        k16 = qkv_ref[0, :, (GDN_HEADS + hd) * GDN_DK:(GDN_HEADS + hd + 1) * GDN_DK]
        v16 = qkv_ref[0, :, (2 * GDN_HEADS + hd) * GDN_DK:(2 * GDN_HEADS + hd + 1) * GDN_DK]
        k = k16.astype(F32)
        decay = jnp.where(incl, jnp.exp(jnp.where(incl, gc - gcr, 0.0)), 0.0)
        kb = k * beta
        a = jnp.where(strict, _dot_nt(kb.astype(BF16), k16) * decay, 0.0)
        hv.append(dict(gc=gc, gt=gt, beta=beta, q16=q16, k16=k16, v16=v16, k=k, decay=decay, kb=kb, a=a))

    ts = [eye - h["a"] for h in hv]
    pws = [_mm3(h["a"], h["a"]) for h in hv]
    for step in range(5):
        ts = [t + _mm3(t, pw) for t, pw in zip(ts, pws)]
        if step < 4:
            pws = [_mm3(pw, pw) for pw in pws]

    for hd, h in zip(heads, hv):
        ex = jnp.exp(h["gc"])
        rhs = jnp.concatenate([(h["v16"].astype(F32) * h["beta"]).astype(BF16), (h["kb"] * ex).astype(BF16)], axis=1)
        uw = _dot(ts[hd].astype(BF16), rhs)
        h["u"] = uw[:, :GDN_DK]
        h["w"] = uw[:, GDN_DK:].astype(BF16)
        h["intra"] = jnp.where(incl, _dot_nt(h["q16"], h["k16"]) * h["decay"], 0.0).astype(BF16)
        h["qd"] = (h["q16"].astype(F32) * ex).astype(BF16)
        h["kd"] = (h["k"] * jnp.exp(h["gt"] - h["gc"])).astype(BF16)

    ss = [state[hd] for hd in heads]
    order = range(nc - 1, -1, -1) if rev else range(nc)
    for cc in order:
        r0 = cc * c
        sbs = [s.astype(BF16) for s in ss]
        vns = [h["u"][r0:r0 + c] - _dot(h["w"][r0:r0 + c], sb) for h, sb in zip(hv, sbs)]
        vn16s = [vn.astype(BF16) for vn in vns]
        for hd, h in zip(heads, hv):
            o_c = _dot(h["qd"][r0:r0 + c], sbs[hd]) + _dot(h["intra"][r0:r0 + c, r0:r0 + c], vn16s[hd])
            o_ref[0, r0:r0 + c, hd * GDN_DK:(hd + 1) * GDN_DK] = o_c
        ss = [s * jnp.exp(h["gt"][r0:r0 + 1, :]) + _dot_tn(h["kd"][r0:r0 + c], vn16)
              for s, h, vn16 in zip(ss, hv, vn16s)]
    for hd in heads:
        state[hd] = ss[hd]

    @pl.when(i == pl.num_programs(1) - 1)
    def _():
        sout_ref[0] = state[...]


def _gdn_scan(qkv, gp, s0, rev):
    b, l, _ = qkv.shape
    n = GDN_GROUP
    g = l // n
    hdv = GDN_HEADS * GDN_DK
    if rev:
        idx = lambda i, j: (i, g - 1 - j, 0)
    else:
        idx = lambda i, j: (i, j, 0)
    return pl.pallas_call(
        functools.partial(_gdn_kernel, rev=rev),
        grid=(b, g),
        in_specs=[pl.BlockSpec((1, n, 3 * hdv), idx),
                  pl.BlockSpec((1, n, LANES), idx),
                  pl.BlockSpec((1, GDN_HEADS, GDN_DK, GDN_DK), lambda i, j: (i, 0, 0, 0))],
        out_specs=[pl.BlockSpec((1, n, hdv), idx),
                   pl.BlockSpec((1, GDN_HEADS, GDN_DK, GDN_DK), lambda i, j: (i, 0, 0, 0))],
        out_shape=[jax.ShapeDtypeStruct((b, l, hdv), F32),
                   jax.ShapeDtypeStruct((b, GDN_HEADS, GDN_DK, GDN_DK), F32)],
        scratch_shapes=[pltpu.VMEM((GDN_HEADS, GDN_DK, GDN_DK), F32)],
        compiler_params=_cparams(("arbitrary", "arbitrary")),
        name="l0_gdn_bwd" if rev else "l0_gdn_fwd",
    )(qkv, gp, s0)


def _gdn_out_kernel(of_ref, ob_ref, z_ref, yc_ref, x_ref, gate_ref, gn_ref, w_ref, o_ref):
    o = of_ref[0] + ob_ref[0]
    z = z_ref[0].astype(F32)
    parts = []
    for hd in range(GDN_HEADS):
        oh = o[:, hd * GDN_DK:(hd + 1) * GDN_DK]
        ms = jnp.mean(oh * oh, axis=-1, keepdims=True)
        parts.append(oh * lax.rsqrt(ms + EPS) * gn_ref[...] * _silu(z[:, hd * GDN_DK:(hd + 1) * GDN_DK]))
    yg = jnp.concatenate(parts, axis=1).astype(BF16)
    cd = CONV_DIM
    y = _dot(yc_ref[0], w_ref[0:cd, :]) + _dot(yg, w_ref[cd:, :])
    o_ref[0] = x_ref[0] + gate_ref[0] * y


def _gdn_out(o_f, o_b, p, yconv, x, gate, gn, w_out):
    b, l, d = x.shape
    tm = min(512, l)
    hdv = GDN_HEADS * GDN_DK
    zblk = (3 * CONV_DIM + 3 * hdv) // hdv
    row = lambda i, j: (i, j, 0)
    return pl.pallas_call(
        _gdn_out_kernel,
        grid=(b, l // tm),
        in_specs=[pl.BlockSpec((1, tm, hdv), row),
                  pl.BlockSpec((1, tm, hdv), row),
                  pl.BlockSpec((1, tm, hdv), lambda i, j: (i, j, zblk)),
                  pl.BlockSpec((1, tm, CONV_DIM), row),
                  pl.BlockSpec((1, tm, d), row),
                  pl.BlockSpec((1, 1, d), lambda i, j: (i, 0, 0)),
                  pl.BlockSpec((1, GDN_DK), lambda i, j: (0, 0)),
                  pl.BlockSpec(w_out.shape, lambda i, j: (0, 0))],
        out_specs=pl.BlockSpec((1, tm, d), row),
        out_shape=jax.ShapeDtypeStruct((b, l, d), F32),
        compiler_params=_cparams(("arbitrary", "arbitrary")),
        name="l0_out_proj",
    )(o_f, o_b, p, yconv, x, gate, gn, w_out)


def _router_kernel(x_ref, sh_ref, sc_ref, nw_ref, rt_ref, hm3_ref, aff_ref, *, tm):
    hm = _modulate_tile(x_ref[0], nw_ref[...], sh_ref[0], sc_ref[0])
    for cc in range(SUBLANES):
        hm3_ref[0, pl.ds(cc, tm, stride=SUBLANES), :] = hm[:, cc * LANES:(cc + 1) * LANES]
    rh, rl = _split2(rt_ref[...])
    hh, hl = _split2(hm)
    logits = _dot_nt(rh, hh) + _dot_nt(rh, hl) + _dot_nt(rl, hh)
    m = jnp.max(logits, axis=0, keepdims=True)
    e = jnp.exp(logits - m)
    aff_ref[0] = e / jnp.sum(e, axis=0, keepdims=True)


def _router(x, shift, scale, nw, router_t):
    b, l, d = x.shape
    tm = min(512, l)
    return pl.pallas_call(
        functools.partial(_router_kernel, tm=tm),
        grid=(b, l // tm),
        in_specs=[pl.BlockSpec((1, tm, d), lambda i, j: (i, j, 0)),
                  pl.BlockSpec((1, 1, d), lambda i, j: (i, 0, 0)),
                  pl.BlockSpec((1, 1, d), lambda i, j: (i, 0, 0)),
                  pl.BlockSpec((1, d), lambda i, j: (0, 0)),
                  pl.BlockSpec((N_EXPERTS, d), lambda i, j: (0, 0))],
        out_specs=[pl.BlockSpec((1, tm * SUBLANES, LANES), lambda i, j: (i, j, 0)),
                   pl.BlockSpec((1, N_EXPERTS, tm), lambda i, j: (i, 0, j))],
        out_shape=[jax.ShapeDtypeStruct((b, l * SUBLANES, LANES), F32),
                   jax.ShapeDtypeStruct((b, N_EXPERTS, l), F32)],
        compiler_params=_cparams(("arbitrary", "arbitrary")),
        name="moe_router",
    )(x, shift, scale, nw, router_t)


def _topk_kernel(aff_ref, idx_ref, gate_ref, *, cap, capp):
    r = TOPK_ROWS
    a3 = aff_ref[0]
    bits3 = pltpu.bitcast(a3, I32)

    def count_ge(cand):
        hit = jnp.where(bits3 >= cand, 1.0, 0.0)
        return jnp.sum(jnp.sum(hit, axis=1, keepdims=True), axis=2, keepdims=True)

    thr = jnp.zeros((N_EXPERTS, 1, 1), I32)
    for bit in range(30, -1, -1):
        cand = thr | (1 << bit)
        thr = jnp.where(count_ge(cand) >= cap, cand, thr)

    li = lax.broadcasted_iota(I32, (LANES, LANES), 0)
    lj = lax.broadcasted_iota(I32, (LANES, LANES), 1)
    tri_incl = jnp.where(li <= lj, 1.0, 0.0).astype(BF16)
    tri_strict_rows = jnp.where(lj < li, 1.0, 0.0).astype(BF16)
    ones8 = jnp.ones((SUBLANES, LANES), BF16)
    lane_row = lax.broadcasted_iota(I32, (capp, LANES), 1).astype(F32)
    slot = lax.broadcasted_iota(I32, (capp, 1), 0).astype(F32)

    for ex in range(N_EXPERTS):
        a = aff_ref[0, ex]
        bits = pltpu.bitcast(a, I32)
        th = thr[ex]
        gt = bits > th
        eq = bits == th
        n_gt = jnp.sum(jnp.sum(jnp.where(gt, 1.0, 0.0), axis=0, keepdims=True), axis=1, keepdims=True)
        need = cap - n_gt
        eq16 = jnp.where(eq, 1.0, 0.0).astype(BF16)
        eq_lc = _dot(eq16, tri_incl)
        eq_cnt = jnp.broadcast_to(eq_lc[:, LANES - 1:LANES], (r, LANES)).astype(BF16)
        eq_off = _dot(tri_strict_rows, eq_cnt)
        take_eq = jnp.logical_and(eq, (eq_off + eq_lc) <= need)
        sel16 = jnp.where(jnp.logical_or(gt, take_eq), 1.0, 0.0).astype(BF16)

        lc = _dot(sel16, tri_incl)
        cnt_row = _dot_nt(ones8, sel16)
        off_incl = _dot(cnt_row.astype(BF16), tri_incl)
        off_excl = off_incl - cnt_row
        jrow = jnp.sum(jnp.where(off_incl[0:1, :] <= slot, 1.0, 0.0), axis=1, keepdims=True)
        onehot = lane_row == jrow
        onehot16 = jnp.where(onehot, 1.0, 0.0).astype(BF16)
        lc_rows = _dot(onehot16, lc.astype(BF16))
        off_s = jnp.sum(jnp.where(onehot, off_excl[0:1, :], 0.0), axis=1, keepdims=True)
        rank = slot - off_s
        pos = jnp.sum(jnp.where(lc_rows <= rank, 1.0, 0.0), axis=1, keepdims=True)
        tok = jrow * float(LANES) + pos
        ah, am, al = _split3(a)
        a_rows = _dot(onehot16, ah) + _dot(onehot16, am) + _dot(onehot16, al)
        gate = jnp.sum(jnp.where(lane_row == pos, a_rows, 0.0), axis=1, keepdims=True)
        tok_t = jnp.broadcast_to(tok, (capp, LANES)).T
        gate_t = jnp.broadcast_to(gate, (capp, LANES)).T
        idx_ref[0, ex] = tok_t[0:1, :].astype(I32)
        gate_ref[0, ex] = gate_t[0:1, :]


def _topk(aff_t, cap):
    b, e, n = aff_t.shape
    npad = TOPK_ROWS * LANES
    capp = max(cap, LANES)
    a = jnp.pad(aff_t, ((0, 0), (0, 0), (0, npad - n)), constant_values=-1.0).reshape(b, e, TOPK_ROWS, LANES)
    idx, gate = pl.pallas_call(
        functools.partial(_topk_kernel, cap=cap, capp=capp),
        grid=(b,),
        in_specs=[pl.BlockSpec((1, e, TOPK_ROWS, LANES), lambda i: (i, 0, 0, 0))],
        out_specs=[pl.BlockSpec((1, e, 1, capp), lambda i: (i, 0, 0, 0)),
                   pl.BlockSpec((1, e, 1, capp), lambda i: (i, 0, 0, 0))],
        out_shape=[jax.ShapeDtypeStruct((b, e, 1, capp), I32),
                   jax.ShapeDtypeStruct((b, e, 1, capp), F32)],
        compiler_params=_cparams(("arbitrary",)),
        name="moe_topk",
    )(a)
    return idx[:, :, 0, :cap], gate[:, :, 0, :cap]


def _gather_kernel(idx_ref, hm3_hbm, xe_ref, stage, sem, *, cap):
    b = pl.program_id(0)

    def row_copy(s):
        t = idx_ref[0, 0, s]
        src = hm3_hbm.at[b, pl.ds(pl.multiple_of(t * SUBLANES, SUBLANES), SUBLANES), :]
        dst = stage.at[pl.ds(pl.multiple_of(s * SUBLANES, SUBLANES), SUBLANES), :]
        return pltpu.make_async_copy(src, dst, sem)

    def issue(g, carry):
        for u in range(SUBLANES):
            row_copy(g * SUBLANES + u).start(priority=u % 2)
        return carry

    lax.fori_loop(0, cap // SUBLANES, issue, 0)
    pltpu.make_async_copy(hm3_hbm.at[b, pl.ds(0, cap * SUBLANES), :], stage, sem).wait()
    for cc in range(SUBLANES):
        xe_ref[0, 0, :, cc * LANES:(cc + 1) * LANES] = stage[pl.ds(cc, cap, stride=SUBLANES), :].astype(BF16)


def _gather(idx, hm3, d):
    b, e, cap = idx.shape
    return pl.pallas_call(
        functools.partial(_gather_kernel, cap=cap),
        grid=(b, e),
        in_specs=[pl.BlockSpec((1, 1, cap), lambda i, j: (i * e + j, 0, 0), memory_space=pltpu.SMEM),
                  pl.BlockSpec(memory_space=pl.ANY)],
        out_specs=pl.BlockSpec((1, 1, cap, d), lambda i, j: (i, j, 0, 0)),
        out_shape=jax.ShapeDtypeStruct((b, e, cap, d), BF16),
        scratch_shapes=[pltpu.VMEM((cap * SUBLANES, LANES), F32), pltpu.SemaphoreType.DMA(())],
        compiler_params=_cparams(("arbitrary", "arbitrary")),
        name="moe_gather",
    )(idx.reshape(b * e, 1, cap), hm3)


def _ffn_kernel(x_ref, wg_ref, wu_ref, wd_ref, y3_ref, acc, *, cap):
    f = pl.program_id(2)
    x = x_ref[0, 0]
    g = _dot(x, wg_ref[0].astype(BF16))
    u = _dot(x, wu_ref[0].astype(BF16))
    h = (_silu(g) * u).astype(BF16)
    part = _dot(h, wd_ref[0].astype(BF16))

    @pl.when(f == 0)
    def _():
        acc[...] = part

    @pl.when(f > 0)
    def _():
        acc[...] += part

    @pl.when(f == pl.num_programs(2) - 1)
    def _():
        for cc in range(SUBLANES):
            y3_ref[0, 0, pl.ds(cc, cap, stride=SUBLANES), :] = acc[:, cc * LANES:(cc + 1) * LANES]


def _ffn(xe, wg, wu, wd):
    b, e, cap, d = xe.shape
    ff = wg.shape[2]
    tf = 512
    return pl.pallas_call(
        functools.partial(_ffn_kernel, cap=cap),
        grid=(b, e, ff // tf),
        in_specs=[pl.BlockSpec((1, 1, cap, d), lambda i, j, k: (i, j, 0, 0)),
                  pl.BlockSpec((1, d, tf), lambda i, j, k: (j, 0, k)),
                  pl.BlockSpec((1, d, tf), lambda i, j, k: (j, 0, k)),
                  pl.BlockSpec((1, tf, d), lambda i, j, k: (j, k, 0))],
        out_specs=pl.BlockSpec((1, 1, cap * SUBLANES, LANES), lambda i, j, k: (i, j, 0, 0)),
        out_shape=jax.ShapeDtypeStruct((b, e, cap * SUBLANES, LANES), F32),
        scratch_shapes=[pltpu.VMEM((cap, d), F32)],
        compiler_params=_cparams(("arbitrary", "arbitrary", "arbitrary")),
        name="moe_ffn",
    )(xe, wg, wu, wd)


def _combine_kernel(idx_ref, gate_ref, y3_ref, out_hbm, acc, sem, *, cap, unroll):
    b = pl.program_id(0)
    e = pl.program_id(1)

    @pl.when(e == 0)
    def _():
        acc[...] = jnp.zeros_like(acc)

    def body(g, carry):
        base = g * unroll
        rows = []
        for u in range(unroll):
            s = base + u
            t = pl.multiple_of(idx_ref[0, 0, s] * SUBLANES, SUBLANES)
            y = y3_ref[0, 0, pl.ds(pl.multiple_of(s * SUBLANES, SUBLANES), SUBLANES), :]
            rows.append((t, acc[pl.ds(t, SUBLANES), :] + gate_ref[0, 0, s] * y))
        for t, val in rows:
            acc[pl.ds(t, SUBLANES), :] = val
        return carry

    lax.fori_loop(0, cap // unroll, body, 0)

    @pl.when(e == pl.num_programs(1) - 1)
    def _():
        cp = pltpu.make_async_copy(acc, out_hbm.at[b], sem)
        cp.start()
        cp.wait()


def _combine(idx, gate, y3, l):
    b, e, cap = idx.shape
    return pl.pallas_call(
        functools.partial(_combine_kernel, cap=cap, unroll=4),
        grid=(b, e),
        in_specs=[pl.BlockSpec((1, 1, cap), lambda i, j: (i * e + j, 0, 0), memory_space=pltpu.SMEM),
                  pl.BlockSpec((1, 1, cap), lambda i, j: (i * e + j, 0, 0), memory_space=pltpu.SMEM),
                  pl.BlockSpec((1, 1, cap * SUBLANES, LANES), lambda i, j: (i, j, 0, 0))],
        out_specs=pl.BlockSpec(memory_space=pl.ANY),
        out_shape=jax.ShapeDtypeStruct((b, l * SUBLANES, LANES), F32),
        scratch_shapes=[pltpu.VMEM((l * SUBLANES, LANES), F32), pltpu.SemaphoreType.DMA(())],
        compiler_params=_cparams(("arbitrary", "arbitrary")),
        name="moe_combine",
    )(idx.reshape(b * e, 1, cap), gate.reshape(b * e, 1, cap), y3)


def _moe_resid_kernel(x_ref, acc3_ref, gate_ref, fw_ref, o_ref, *, tm, final):
    parts = [acc3_ref[0, pl.ds(cc, tm, stride=SUBLANES), :] for cc in range(SUBLANES)]
    y = x_ref[0] + gate_ref[0] * jnp.concatenate(parts, axis=1)
    if final:
        ms = jnp.mean(y * y, axis=-1, keepdims=True)
        y = y * lax.rsqrt(ms + EPS) * fw_ref[...]
    o_ref[0] = y


def _moe_resid(x, acc3, gate, final_w, final):
    b, l, d = x.shape
    tm = min(512, l)
    return pl.pallas_call(
        functools.partial(_moe_resid_kernel, tm=tm, final=final),
        grid=(b, l // tm),
        in_specs=[pl.BlockSpec((1, tm, d), lambda i, j: (i, j, 0)),
                  pl.BlockSpec((1, tm * SUBLANES, LANES), lambda i, j: (i, j, 0)),
                  pl.BlockSpec((1, 1, d), lambda i, j: (i, 0, 0)),
                  pl.BlockSpec((1, d), lambda i, j: (0, 0))],
        out_specs=pl.BlockSpec((1, tm, d), lambda i, j: (i, j, 0)),
        out_shape=jax.ShapeDtypeStruct((b, l, d), F32),
        compiler_params=_cparams(("arbitrary", "arbitrary")),
        name="moe_resid",
    )(x, acc3, gate, final_w)


def _moe(x, shift, scale, gate, nw, router_t, wg, wu, wd, final_w, final):
    b, l, d = x.shape
    cap = CAPACITY_FACTOR * l // N_EXPERTS
    hm3, aff_t = _router(x, shift, scale, nw, router_t)
    idx, gates = _topk(aff_t, cap)
    xe = _gather(idx, hm3, d)
    y3 = _ffn(xe, wg, wu, wd)
    acc3 = _combine(idx, gates, y3, l)
    return _moe_resid(x, acc3, gate, final_w, final)


def _qkv_kernel(x_ref, sh_ref, sc_ref, nw_ref, w_ref, cos_ref, sin_ref, q_ref, k_ref, vt_ref, *, rope, qscale):
    h = _modulate_tile(x_ref[0], nw_ref[...], sh_ref[0], sc_ref[0]).astype(BF16)
    dq = q_ref.shape[2]
    chunk = 512
    rep = chunk // LANES
    if rope:
        cos = jnp.concatenate([cos_ref[...]] * rep, axis=1)
        sin = jnp.concatenate([sin_ref[...]] * rep, axis=1)
        lane = lax.broadcasted_iota(I32, (x_ref.shape[1], chunk), 1)
        first = (lane % 32) < 16
    hd = 2 * DIFF_DH
    for which, ref in enumerate((q_ref, k_ref, vt_ref)):
        for c0 in range(0, dq, chunk):
            y = _dot(h, w_ref[:, which * dq + c0:which * dq + c0 + chunk])
            if rope and which < 2:
                partner = jnp.where(first, pltpu.roll(y, chunk - 16, axis=1), pltpu.roll(y, 16, axis=1))
                y = y * cos + partner * sin
            if which == 0:
                y = y * qscale
            if which < 2:
                ref[0, :, c0:c0 + chunk] = y.astype(BF16)
            else:
                for c1 in range(0, chunk, hd):
                    ref[0, c0 + c1:c0 + c1 + hd, :] = y[:, c1:c1 + hd].T.astype(BF16)


def _qkv_proj(x, shift, scale, nw, w, cos_t, sin_t, rope):
    b, l, d = x.shape
    tm = min(512, l)
    dq = w.shape[1] // 3
    qscale = (DIFF_DH ** -0.5) * LOG2E
    row = lambda i, j: (i, j, 0)
    out = jax.ShapeDtypeStruct((b, l, dq), BF16)
    return pl.pallas_call(
        functools.partial(_qkv_kernel, rope=rope, qscale=qscale),
        grid=(b, l // tm),
        in_specs=[pl.BlockSpec((1, tm, d), row),
                  pl.BlockSpec((1, 1, d), lambda i, j: (i, 0, 0)),
                  pl.BlockSpec((1, 1, d), lambda i, j: (i, 0, 0)),
                  pl.BlockSpec((1, d), lambda i, j: (0, 0)),
                  pl.BlockSpec(w.shape, lambda i, j: (0, 0)),
                  pl.BlockSpec((tm, LANES), lambda i, j: (j, 0)),
                  pl.BlockSpec((tm, LANES), lambda i, j: (j, 0))],
        out_specs=[pl.BlockSpec((1, tm, dq), row), pl.BlockSpec((1, tm, dq), row),
                   pl.BlockSpec((1, dq, tm), lambda i, j: (i, 0, j))],
        out_shape=[out, out, jax.ShapeDtypeStruct((b, dq, l), BF16)],
        compiler_params=_cparams(("arbitrary", "arbitrary")),
        name="l1_qkv_rope" if rope else "l1_qkv_ctx",
    )(x, shift, scale, nw, w, cos_t, sin_t)


def _rope_tables(l):
    t = np.arange(l)
    n_freq = DIFF_DH // 4
    inv_freq = ROPE_BASE ** (-np.arange(n_freq, dtype=np.float32) / n_freq)
    ang_row = (t // GRID_W).astype(np.float32)[:, None] * inv_freq
    ang_col = (t % GRID_W).astype(np.float32)[:, None] * inv_freq
    ang = np.concatenate([ang_row, ang_row, ang_col, ang_col], axis=1)
    sign = np.concatenate([-np.ones(n_freq), np.ones(n_freq)] * 2).astype(np.float32)
    cos = np.cos(ang).astype(np.float32)
    sin = (np.sin(ang) * sign).astype(np.float32)
    return jnp.asarray(np.concatenate([cos, cos], axis=1)), jnp.asarray(np.concatenate([sin, sin], axis=1))


def _attn_kernel(q_ref, k_ref, vt_ref, lam_ref, sub_ref, o_ref, *, tq, tk, lam_init):
    lq = q_ref.shape[1]
    lk = k_ref.shape[1]
    dv = vt_ref.shape[1]
    nk = lk // tk
    lp = lam_ref[...]
    lam = (jnp.exp(jnp.sum(lp[0:1] * lp[1:2], axis=1, keepdims=True))
           - jnp.exp(jnp.sum(lp[2:3] * lp[3:4], axis=1, keepdims=True)) + lam_init)

    def update(s, vt, m, l, acc):
        mn = jnp.maximum(m, jnp.max(s, axis=0, keepdims=True))
        alpha = jnp.exp2(m - mn)
        p = jnp.exp2(s - mn)
        l = alpha * l + jnp.sum(p, axis=0, keepdims=True)
        acc = alpha * acc + _dot(vt, p.astype(BF16))
        return mn, l, acc

    def q_tile(i, carry):
        r0 = pl.multiple_of(i * tq, tq)
        q = q_ref[0, pl.ds(r0, tq), :]
        lane = lax.broadcasted_iota(I32, q.shape, 1)
        zero = jnp.zeros_like(q)
        q1 = jnp.where(lane < DIFF_DH, q, zero)
        q2 = jnp.where(lane >= DIFF_DH, q, zero)
        m1 = m2 = jnp.full((1, tq), -1e30, F32)
        l1 = l2 = jnp.zeros((1, tq), F32)
        a1 = a2 = jnp.zeros((dv, tq), F32)
        s1 = _dot_nt(k_ref[0, 0:tk, :], q1)
        s2 = _dot_nt(k_ref[0, 0:tk, :], q2)
        for j in range(nk):
            vt = vt_ref[0, :, j * tk:(j + 1) * tk]
            if j + 1 < nk:
                kn = k_ref[0, (j + 1) * tk:(j + 2) * tk, :]
                s1n = _dot_nt(kn, q1)
            m1, l1, a1 = update(s1, vt, m1, l1, a1)
            if j + 1 < nk:
                s2n = _dot_nt(kn, q2)
            m2, l2, a2 = update(s2, vt, m2, l2, a2)
            if j + 1 < nk:
                s1, s2 = s1n, s2n
        o = a1 / l1 - lam * (a2 / l2)
        ms = jnp.mean(o * o, axis=0, keepdims=True)
        o = (o * lax.rsqrt(ms + EPS)).T * (sub_ref[...] * (1.0 - lam_init))
        o_ref[0, pl.ds(r0, tq), :] = o.astype(BF16)
        return carry

    lax.fori_loop(0, lq // tq, q_tile, 0)


def _attention(q, k_all, vt_all, lam_params, subln, lam_init):
    b, lq, dq = q.shape
    lk = k_all.shape[1]
    hd = 2 * DIFF_DH
    tq = min(256, lq)
    tk = 768 if lk % 768 == 0 else 256
    return pl.pallas_call(
        functools.partial(_attn_kernel, tq=tq, tk=tk, lam_init=lam_init),
        grid=(b, DIFF_HEADS),
        in_specs=[pl.BlockSpec((1, lq, hd), lambda i, h: (i, 0, h)),
                  pl.BlockSpec((1, lk, hd), lambda i, h: (i, 0, h)),
                  pl.BlockSpec((1, hd, lk), lambda i, h: (i, h, 0)),
                  pl.BlockSpec(lam_params.shape, lambda i, h: (0, 0)),
                  pl.BlockSpec((1, hd), lambda i, h: (0, 0))],
        out_specs=pl.BlockSpec((1, lq, hd), lambda i, h: (i, 0, h)),
        out_shape=jax.ShapeDtypeStruct((b, lq, dq), BF16),
        compiler_params=_cparams(("arbitrary", "arbitrary")),
        name="l1_diff_attn",
    )(q, k_all, vt_all, lam_params, subln)


def _resid_matmul_kernel(a_ref, w_ref, x_ref, gate_ref, o_ref):
    o_ref[0] = x_ref[0] + gate_ref[0] * _dot(a_ref[0], w_ref[...])


def _resid_matmul(a, w, x, gate):
    b, l, d = x.shape
    tm = min(512, l)
    row = lambda i, j: (i, j, 0)
    return pl.pallas_call(
        _resid_matmul_kernel,
        grid=(b, l // tm),
        in_specs=[pl.BlockSpec((1, tm, a.shape[2]), row),
                  pl.BlockSpec(w.shape, lambda i, j: (0, 0)),
                  pl.BlockSpec((1, tm, d), row),
                  pl.BlockSpec((1, 1, d), lambda i, j: (i, 0, 0))],
        out_specs=pl.BlockSpec((1, tm, d), row),
        out_shape=jax.ShapeDtypeStruct((b, l, d), F32),
        compiler_params=_cparams(("arbitrary", "arbitrary")),
        name="l1_out_proj",
    )(a, w, x, gate)


def _mod_vectors(c, c_ctx, w, b):
    bn, d = c.shape
    cond = jnp.concatenate([c, c_ctx[None], jnp.zeros((8 - bn - 1, d), F32)], axis=0)
    m = _adaln(cond, w, b)
    lat = [m[:bn, i * d:(i + 1) * d][:, None, :] for i in range(6)]
    ctx = [jnp.broadcast_to(m[bn:bn + 1, i * d:(i + 1) * d][:, None, :], (bn, 1, d)) for i in range(6)]
    return lat, ctx


def _row(v):
    return v.reshape(1, -1)


def kernel(x, c, ctx, c_ctx, l0_mod_w, l0_mod_b, l0_norm1, l0_w_in, l0_conv_a, l0_conv_qkv, l0_a_log, l0_dt_bias, l0_gdn_norm, l0_w_out, l0_norm2, l0_router, l0_exp_gate, l0_exp_up, l0_exp_down, l1_mod_w, l1_mod_b, l1_norm1, l1_w_qkv, l1_lambda_q1, l1_lambda_k1, l1_lambda_q2, l1_lambda_k2, l1_subln, l1_w_o, l1_norm2, l1_router, l1_exp_gate, l1_exp_up, l1_exp_down, final_norm):
    bn, seq, d = x.shape
    n_main = 3 * CONV_DIM + 4 * GDN_HEADS * GDN_DK
    n_gate = 4 * GDN_HEADS

    m, mc = _mod_vectors(c, c_ctx, l0_mod_w, l0_mod_b)
    w_in = jnp.concatenate([l0_w_in, jnp.zeros((d, LANES - n_gate), F32)], axis=1).astype(BF16)
    w_out = l0_w_out.astype(BF16)
    pad = jnp.zeros((LANES - 2 * GDN_HEADS,), F32)
    alog_vec = jnp.concatenate([l0_a_log.reshape(-1), pad]).reshape(1, LANES)
    dt_vec = jnp.concatenate([l0_dt_bias.reshape(-1), pad]).reshape(1, LANES)
    router0_t = l0_router.T

    def mixer0(tok, mod, s0_f, s0_b):
        p, gates = _in_proj(tok, mod[0], mod[1], _row(l0_norm1), w_in, n_main)
        yconv, qkv, gp = _conv_prep(p, gates, l0_conv_a, l0_conv_qkv, alog_vec, dt_vec)
        o_f, s_f = _gdn_scan(qkv, gp, s0_f, rev=False)
        o_b, s_b = _gdn_scan(qkv, gp, s0_b, rev=True)
        out = _gdn_out(o_f, o_b, p, yconv, tok, mod[2], _row(l0_gdn_norm), w_out)
        return out, s_f, s_b

    s0 = jnp.zeros((bn, GDN_HEADS, GDN_DK, GDN_DK), F32)
    ctx1, s_f, s_b = mixer0(ctx, mc, s0, s0)
    x1, _, _ = mixer0(x, m, s_f, s_b)
    moe0 = functools.partial(_moe, nw=_row(l0_norm2), router_t=router0_t, wg=l0_exp_gate, wu=l0_exp_up,
                             wd=l0_exp_down, final_w=_row(final_norm), final=False)
    x2 = moe0(x1, m[3], m[4], m[5])
    ctx2 = moe0(ctx1, mc[3], mc[4], mc[5])

    m, mc = _mod_vectors(c, c_ctx, l1_mod_w, l1_mod_b)
    lam_init = 0.8 - 0.6 * math.exp(-0.3 * 1)
    w_qkv = l1_w_qkv.astype(BF16)
    cos_t, sin_t = _rope_tables(seq)
    q, k_lat, vt_lat = _qkv_proj(x2, m[0], m[1], _row(l1_norm1), w_qkv, cos_t, sin_t, rope=True)
    lc = ctx2.shape[1]
    _, k_ctx, vt_ctx = _qkv_proj(ctx2, mc[0], mc[1], _row(l1_norm1), w_qkv, cos_t[:lc], sin_t[:lc], rope=False)
    k_all = jnp.concatenate([k_ctx, k_lat], axis=1)
    vt_all = jnp.concatenate([vt_ctx, vt_lat], axis=2)
    lam_params = jnp.stack([l1_lambda_q1, l1_lambda_k1, l1_lambda_q2, l1_lambda_k2])
    o = _attention(q, k_all, vt_all, lam_params, _row(l1_subln), lam_init)
    x3 = _resid_matmul(o, l1_w_o.astype(BF16), x2, m[2])
    return _moe(x3, m[3], m[4], m[5], _row(l1_norm2), l1_router.T, l1_exp_gate, l1_exp_up, l1_exp_down,
                _row(final_norm), True)
```
